```python
import jax, jax.numpy as jnp
from jax import lax
import numpy as np

D_MODEL = 2048
BATCH = 2
SEQ = 8192
DEPTH = 2

N_BRANCH = 4
MIX_W = D_MODEL // 4
POOL_GROUPS = 4
POOL_WINDOWS = (2, 4, 8, 16)
POOL_GC = MIX_W // POOL_GROUPS
POOL_OUT = D_MODEL // POOL_GROUPS
CONF_W = MIX_W
CONF_KERNEL = 31
SC_W = MIX_W
SC_KERNEL = 3
FN_GROUPS = 4
FN_GC = MIX_W // FN_GROUPS
OFF_A = 0
OFF_B = OFF_A + MIX_W
OFF_C = OFF_B + 2 * CONF_W
OFF_D = OFF_C + 3 * SC_W
OFF_G = OFF_D + MIX_W
N_IN = OFF_G + N_BRANCH * D_MODEL
N_GROUPS = 4
EXPERTS_PER_GROUP = 8
N_EXPERTS = N_GROUPS * EXPERTS_PER_GROUP
TOP_K = 2
D_EXPERT = D_MODEL // 4
EXPERT_BLOCK = 128
EPS = 1e-6

kernel_name = "hybrid_pool_conformer_shortconv_fnet_hmoe_adaln"


def rmsnorm(x, g):
    x32 = x.astype(jnp.float32)
    y = x32 * lax.rsqrt(jnp.mean(x32 * x32, axis=-1, keepdims=True) + EPS)
    return (y * g.astype(jnp.float32)).astype(x.dtype)


def layernorm(x, g, b):
    x32 = x.astype(jnp.float32)
    mu = jnp.mean(x32, axis=-1, keepdims=True)
    var = jnp.mean(jnp.square(x32 - mu), axis=-1, keepdims=True)
    y = (x32 - mu) * lax.rsqrt(var + EPS) * g.astype(jnp.float32) + b.astype(jnp.float32)
    return y.astype(x.dtype)


def modulate(h, shift, scale):
    return h * (1.0 + scale[:, None, :]) + shift[:, None, :]


def depthwise_conv(v, w):
    k, ch = w.shape
    return lax.conv_general_dilated(
        v, w[:, None, :].astype(v.dtype), window_strides=(1,),
        padding=[(k // 2, k // 2)], dimension_numbers=("NWC", "WIO", "NWC"),
        feature_group_count=ch)


def pool_mixer(u, pool_w, pool_scale):
    b, s, _ = u.shape
    u32 = u.astype(jnp.float32).reshape(b, s, POOL_GROUPS, POOL_GC)
    cs = jnp.concatenate([jnp.zeros((b, 1, POOL_GROUPS, POOL_GC), jnp.float32),
                          jnp.cumsum(u32, axis=1)], axis=1)
    t = jnp.arange(s)
    outs = []
    for gi, w in enumerate(POOL_WINDOWS):
        left = w // 2
        right = w - 1 - left
        lo = jnp.clip(t - left, 0, s)
        hi = jnp.clip(t + right + 1, 0, s)
        cnt = (hi - lo).astype(jnp.float32)[None, :, None]
        outs.append((cs[:, hi, gi] - cs[:, lo, gi]) / cnt - u32[:, :, gi])
    p = jnp.stack(outs, axis=2).astype(u.dtype)
    y = jnp.einsum("bsgc,gcd->bsgd", p, pool_w).reshape(b, s, D_MODEL)
    return y * pool_scale


def conformer_conv(u, dw_w, dw_b, ln_g, ln_b, w_out):
    a, gate = jnp.split(u, 2, axis=-1)
    v = a * jax.nn.sigmoid(gate)
    v = depthwise_conv(v, dw_w) + dw_b
    v = jax.nn.silu(layernorm(v, ln_g, ln_b))
    return v @ w_out


def short_conv(u, conv_w, w_out):
    bg, cg, xv = jnp.split(u, 3, axis=-1)
    z = depthwise_conv(cg * xv, conv_w)
    return (bg * z) @ w_out


def fourier_mixer(u, w_out):
    b, s, _ = u.shape
    z = u.astype(jnp.float32).reshape(b, s, FN_GROUPS, FN_GC)
    f = jnp.fft.fft2(z, axes=(1, 3), norm="ortho").real
    return f.reshape(b, s, MIX_W).astype(u.dtype) @ w_out


def hier_moe(h, rg_w, rg_b, re_w, re_b, w1, w3, w2):
    b, s, d = h.shape
    t_tok = b * s
    ht = h.reshape(t_tok, d)
    pg = jax.nn.softmax((ht @ rg_w + rg_b).astype(jnp.float32), axis=-1)
    pg_top, g_idx = lax.top_k(pg, 1)
    le = (ht @ re_w + re_b).astype(jnp.float32).reshape(t_tok, N_GROUPS, EXPERTS_PER_GROUP)
    le = jnp.take_along_axis(le, g_idx[:, :, None], axis=1)[:, 0]
    pe = jax.nn.softmax(le, axis=-1)
    pe_top, e_idx = lax.top_k(pe, TOP_K)
    pe_top = pe_top / jnp.sum(pe_top, axis=-1, keepdims=True)
    weights = pg_top * pe_top
    experts = g_idx * EXPERTS_PER_GROUP + e_idx
    m = t_tok * TOP_K
    a_exp = experts.reshape(m)
    a_tok = jnp.repeat(jnp.arange(t_tok, dtype=jnp.int32), TOP_K)
    a_w = weights.reshape(m)
    order = jnp.argsort(a_exp)
    s_exp, s_tok, s_w = a_exp[order], a_tok[order], a_w[order]
    counts = jnp.bincount(a_exp, length=N_EXPERTS)
    starts = jnp.cumsum(counts) - counts
    pcounts = (counts + EXPERT_BLOCK - 1) // EXPERT_BLOCK * EXPERT_BLOCK
    pends = jnp.cumsum(pcounts)
    pstarts = pends - pcounts
    dest = pstarts[s_exp] + (jnp.arange(m) - starts[s_exp])
    n_rows = m + N_EXPERTS * EXPERT_BLOCK
    n_blocks = n_rows // EXPERT_BLOCK
    buf_tok = jnp.zeros((n_rows,), jnp.int32).at[dest].set(s_tok)
    buf_w = jnp.zeros((n_rows,), jnp.float32).at[dest].set(s_w)
    block_exp = jnp.clip(jnp.searchsorted(pends, jnp.arange(n_blocks) * EXPERT_BLOCK, side="right"),
                         0, N_EXPERTS - 1)
    xb = ht[buf_tok].reshape(n_blocks, EXPERT_BLOCK, d)

    def run_block(args):
        xblk, e = args
        return (jax.nn.silu(xblk @ w1[e]) * (xblk @ w3[e])) @ w2[e]

    yb = lax.map(run_block, (xb, block_exp)).reshape(n_rows, d)
    out = jnp.zeros((t_tok, d), jnp.float32).at[buf_tok].add(yb.astype(jnp.float32) * buf_w[:, None])
    return out.astype(h.dtype).reshape(b, s, d)


def setup_inputs(seed: int = 0) -> dict:
    key = jax.random.key(seed)
    ks = iter(jax.random.split(key, 32))
    L, D = DEPTH, D_MODEL

    def nrm(shape, scale):
        return jax.random.normal(next(ks), shape, jnp.float32) * scale

    return {
        "x": nrm((BATCH, SEQ, D), 1.0),
        "c": nrm((BATCH, D), 1.0),
        "ada_w": nrm((L, D, 6 * D), 0.5 * D ** -0.5),
        "ada_b": nrm((L, 6 * D), 0.01),
        "norm1_g": 1.0 + nrm((L, D), 0.02),
        "w_in": nrm((L, D, N_IN), D ** -0.5),
        "pool_w": nrm((L, POOL_GROUPS, POOL_GC, POOL_OUT), POOL_GC ** -0.5),
        "pool_scale": 1.0 + nrm((L, D), 0.02),
        "conf_dw_w": nrm((L, CONF_KERNEL, CONF_W), CONF_KERNEL ** -0.5),
        "conf_dw_b": nrm((L, CONF_W), 0.01),
        "conf_ln_g": 1.0 + nrm((L, CONF_W), 0.02),
        "conf_ln_b": nrm((L, CONF_W), 0.01),
        "conf_w_out": nrm((L, CONF_W, D), CONF_W ** -0.5),
        "sc_conv_w": nrm((L, SC_KERNEL, SC_W), SC_KERNEL ** -0.5),
        "sc_w_out": nrm((L, SC_W, D), SC_W ** -0.5),
        "fnet_w_out": nrm((L, MIX_W, D), MIX_W ** -0.5),
        "w_o": nrm((L, D, D), D ** -0.5),
        "norm2_g": 1.0 + nrm((L, D), 0.02),
        "rg_w": nrm((L, D, N_GROUPS), D ** -0.5),
        "rg_b": nrm((L, N_GROUPS), 0.01),
        "re_w": nrm((L, D, N_EXPERTS), D ** -0.5),
        "re_b": nrm((L, N_EXPERTS), 0.01),
        "exp_w1": nrm((L, N_EXPERTS, D, D_EXPERT), D ** -0.5),
        "exp_w3": nrm((L, N_EXPERTS, D, D_EXPERT), D ** -0.5),
        "exp_w2": nrm((L, N_EXPERTS, D_EXPERT, D), D_EXPERT ** -0.5),
        "final_g": 1.0 + nrm((D,), 0.02),
    }


def reference(x, c, ada_w, ada_b, norm1_g, w_in, pool_w, pool_scale, conf_dw_w, conf_dw_b,
              conf_ln_g, conf_ln_b, conf_w_out, sc_conv_w, sc_w_out, fnet_w_out, w_o, norm2_g,
              rg_w, rg_b, re_w, re_b, exp_w1, exp_w3, exp_w2, final_g):
    b, s, d = x.shape
    for l in range(DEPTH):
        mod = c @ ada_w[l] + ada_b[l]
        sh1, sc1, gt1, sh2, sc2, gt2 = jnp.split(mod, 6, axis=-1)
        h = modulate(rmsnorm(x, norm1_g[l]), sh1, sc1)
        u = h @ w_in[l]
        y_a = pool_mixer(u[..., OFF_A:OFF_B], pool_w[l], pool_scale[l])
        y_b = conformer_conv(u[..., OFF_B:OFF_C], conf_dw_w[l], conf_dw_b[l],
                             conf_ln_g[l], conf_ln_b[l], conf_w_out[l])
        y_c = short_conv(u[..., OFF_C:OFF_D], sc_conv_w[l], sc_w_out[l])
        y_d = fourier_mixer(u[..., OFF_D:OFF_G], fnet_w_out[l])
        gates = jax.nn.sigmoid(u[..., OFF_G:]).reshape(b, s, N_BRANCH, d)
        merged = (gates[:, :, 0] * y_a + gates[:, :, 1] * y_b
                  + gates[:, :, 2] * y_c + gates[:, :, 3] * y_d)
        x = x + gt1[:, None, :] * (merged @ w_o[l])
        h2 = modulate(rmsnorm(x, norm2_g[l]), sh2, sc2)
        x = x + gt2[:, None, :] * hier_moe(h2, rg_w[l], rg_b[l], re_w[l], re_b[l],
                                           exp_w1[l], exp_w3[l], exp_w2[l])
    return rmsnorm(x, final_g)
```

```python
import functools

import numpy as np
import jax
import jax.numpy as jnp
from jax import lax
from jax.experimental import pallas as pl
from jax.experimental.pallas import tpu as pltpu

F32 = jnp.float32
BF16 = jnp.bfloat16
U32 = jnp.uint32
I32 = jnp.int32

LANES = 128
EPS = 1e-6
N_BRANCH = 4
POOL_WINDOWS = (2, 4, 8, 16)
CONF_KERNEL = 31
SC_KERNEL = 3
N_GROUPS = 4
EXPERTS_PER_GROUP = 8
N_EXPERTS = N_GROUPS * EXPERTS_PER_GROUP
HALO = 64
EXPERT_ROWS = 256
FFT_N1 = 128
VMEM_LIMIT = 56 * 1024 * 1024
NEG = -1e30


def _cparams(*sem):
    return pltpu.CompilerParams(dimension_semantics=sem, vmem_limit_bytes=VMEM_LIMIT)


def _sigmoid(v):
    return 1.0 / (1.0 + jnp.exp(-v))


def _norm_modulate(x, g, sh, sc):
    ms = jnp.mean(x * x, axis=-1, keepdims=True)
    return (x * lax.rsqrt(ms + EPS)) * (g * (1.0 + sc)) + sh


def _ada_kernel(cb_ref, w_ref, b_ref, o_ref, *, nb, tn):
    w = w_ref[0]
    for b in range(nb):
        cbt = jnp.tile(cb_ref[b], (1, tn // LANES))
        o_ref[0, b:b + 1, :] = jnp.sum(w * cbt, axis=0, keepdims=True) + b_ref[0]


def _ada_mod(c, ada_w, ada_b):
    nl, d, n = ada_w.shape
    nb = c.shape[0]
    tn = 512
    cb = jnp.broadcast_to(c[:, :, None], (nb, d, LANES))
    return pl.pallas_call(
        functools.partial(_ada_kernel, nb=nb, tn=tn),
        grid=(nl, n // tn),
        in_specs=[
            pl.BlockSpec((nb, d, LANES), lambda l, j: (0, 0, 0)),
            pl.BlockSpec((1, d, tn), lambda l, j: (l, 0, j)),
            pl.BlockSpec((1, 1, tn), lambda l, j: (l, 0, j)),
        ],
        out_specs=pl.BlockSpec((1, nb, tn), lambda l, j: (l, 0, j)),
        out_shape=jax.ShapeDtypeStruct((nl, nb, n), F32),
        compiler_params=_cparams("arbitrary", "arbitrary"),
        name="ada_mod",
    )(cb, ada_w, ada_b.reshape(nl, 1, n))


def _inproj_kernel(x_ref, g_ref, sh_ref, sc_ref, w_ref, u_ref, h_ref):
    @pl.when(pl.program_id(1) == 0)
    def _():
        h_ref[...] = _norm_modulate(x_ref[...], g_ref[...], sh_ref[0], sc_ref[0]).astype(BF16)

    u_ref[...] = jnp.dot(h_ref[...], w_ref[...], preferred_element_type=F32).astype(BF16)


def _inproj(x2, g, sh, sc, w_mix, n, seq):
    t, d = x2.shape
    tm, tn = 1024, 512
    tps = seq // tm
    return pl.pallas_call(
        _inproj_kernel,
        grid=(t // tm, n // tn),
        in_specs=[
            pl.BlockSpec((tm, d), lambda i, j: (i, 0)),
            pl.BlockSpec((1, d), lambda i, j: (0, 0)),
            pl.BlockSpec((1, 1, d), lambda i, j: (i // tps, 0, 0)),
            pl.BlockSpec((1, 1, d), lambda i, j: (i // tps, 0, 0)),
            pl.BlockSpec((d, tn), lambda i, j: (0, j)),
        ],
        out_specs=pl.BlockSpec((tm, tn), lambda i, j: (i, j)),
        out_shape=jax.ShapeDtypeStruct((t, n), BF16),
        scratch_shapes=[pltpu.VMEM((tm, d), BF16)],
        compiler_params=_cparams("arbitrary", "arbitrary"),
        name="inproj",
    )(x2, g, sh, sc, w_mix)


def _pool_band(tm):
    bands = np.zeros((len(POOL_WINDOWS), tm, tm + 2 * HALO), np.float32)
    for gi, w in enumerate(POOL_WINDOWS):
        left = w // 2
        for i in range(tm):
            lo = i + HALO - left
            bands[gi, i, lo:lo + w] = 1.0
    return bands


def _mixers_kernel(um_ref, up_ref, un_ref, band_ref, poolw_ref, pscale_ref,
                   dww_ref, dwb_ref, lng_ref, lnb_ref, cwo_ref, scw_ref, swo_ref,
                   ya_ref, yb_ref, yc_ref, ub_ref, vb_ref, pb_ref, cv_ref,
                   *, tm, tps, seq, mw):
    ti = pl.program_id(0) % tps
    rows = tm + 2 * HALO
    ub_ref[0:HALO, :] = jnp.where(ti > 0, up_ref[...], jnp.zeros_like(up_ref))
    ub_ref[HALO:HALO + tm, :] = um_ref[...]
    ub_ref[HALO + tm:rows, :] = jnp.where(ti < tps - 1, un_ref[...], jnp.zeros_like(un_ref))

    pos = (ti * tm + lax.broadcasted_iota(I32, (tm, 1), 0)).astype(F32)
    gc = mw // len(POOL_WINDOWS)
    dout = ya_ref.shape[1] // len(POOL_WINDOWS)
    for gi, w in enumerate(POOL_WINDOWS):
        left = w // 2
        right = w - 1 - left
        lo = jnp.clip(pos - left, 0.0, float(seq))
        hi = jnp.clip(pos + (right + 1), 0.0, float(seq))
        cnt = hi - lo
        ug = ub_ref[:, gi * gc:(gi + 1) * gc]
        wsum = jnp.dot(band_ref[gi], ug, preferred_element_type=F32)
        p = wsum / cnt - ug[HALO:HALO + tm, :].astype(F32)
        y = jnp.dot(p.astype(BF16), poolw_ref[gi], preferred_element_type=F32)
        ya_ref[:, gi * dout:(gi + 1) * dout] = (
            y * pscale_ref[:, gi * dout:(gi + 1) * dout]).astype(BF16)

    a = ub_ref[:, mw:2 * mw].astype(F32)
    gate = ub_ref[:, 2 * mw:3 * mw].astype(F32)
    vb_ref[...] = a * _sigmoid(gate)
    rc = 128
    for r0 in range(0, tm, rc):
        for c0 in range(0, mw, LANES):
            acc = jnp.zeros((rc, LANES), F32)
            for k in range(CONF_KERNEL):
                off = r0 + HALO - CONF_KERNEL // 2 + k
                acc = acc + dww_ref[k:k + 1, c0:c0 + LANES] * vb_ref[off:off + rc, c0:c0 + LANES]
            cv_ref[r0:r0 + rc, c0:c0 + LANES] = acc
    v = cv_ref[...] + dwb_ref[...]
    mu = jnp.mean(v, axis=-1, keepdims=True)
    dv = v - mu
    var = jnp.mean(dv * dv, axis=-1, keepdims=True)
    ln = dv * lax.rsqrt(var + EPS) * lng_ref[...] + lnb_ref[...]
    act = ln * _sigmoid(ln)
    yb_ref[...] = jnp.dot(act.astype(BF16), cwo_ref[...], preferred_element_type=F32).astype(BF16)

    pb_ref[...] = ub_ref[:, 4 * mw:5 * mw].astype(F32) * ub_ref[:, 5 * mw:6 * mw].astype(F32)
    z = jnp.zeros((tm, mw), F32)
    for k in range(SC_KERNEL):
        off = HALO - SC_KERNEL // 2 + k
        z = z + scw_ref[k:k + 1, :] * pb_ref[off:off + tm, :]
    bg = ub_ref[HALO:HALO + tm, 3 * mw:4 * mw].astype(F32)
    yc_ref[...] = jnp.dot((bg * z).astype(BF16), swo_ref[...], preferred_element_type=F32).astype(BF16)


def _mixers(u, band, pool_w, pool_scale, dw_w, dw_b, ln_g, ln_b, conf_wo, sc_w, sc_wo, seq, d):
    t = u.shape[0]
    mw = pool_w.shape[0] * pool_w.shape[1]
    tm = band.shape[1]
    tps = seq // tm
    hb = tm // HALO
    nhalo = t // HALO
    wabc = 6 * mw
    full = lambda shape: pl.BlockSpec(shape, lambda i: (0,) * len(shape))
    out_sds = jax.ShapeDtypeStruct((t, d), BF16)
    return pl.pallas_call(
        functools.partial(_mixers_kernel, tm=tm, tps=tps, seq=seq, mw=mw),
        grid=(t // tm,),
        in_specs=[
            pl.BlockSpec((tm, wabc), lambda i: (i, 0)),
            pl.BlockSpec((HALO, wabc), lambda i: (jnp.maximum(i * hb - 1, 0), 0)),
            pl.BlockSpec((HALO, wabc), lambda i: (jnp.minimum((i + 1) * hb, nhalo - 1), 0)),
            full(band.shape), full(pool_w.shape), full((1, d)),
            full(dw_w.shape), full((1, mw)), full((1, mw)), full((1, mw)),
            full(conf_wo.shape), full(sc_w.shape), full(sc_wo.shape),
        ],
        out_specs=[pl.BlockSpec((tm, d), lambda i: (i, 0))] * 3,
        out_shape=[out_sds] * 3,
        scratch_shapes=[
            pltpu.VMEM((tm + 2 * HALO, wabc), BF16),
            pltpu.VMEM((tm + 2 * HALO, mw), F32),
            pltpu.VMEM((tm + 2 * HALO, mw), F32),
            pltpu.VMEM((tm, mw), F32),
        ],
        compiler_params=_cparams("arbitrary"),
        name="mixers",
    )(u, u, u, band, pool_w, pool_scale.reshape(1, d), dw_w, dw_b.reshape(1, mw),
      ln_g.reshape(1, mw), ln_b.reshape(1, mw), conf_wo, sc_w, sc_wo)


def _dft_tables(seq, gc):
    n1, n2 = FFT_N1, seq // FFT_N1
    k2 = np.arange(n2)[:, None] * np.arange(n2)[None, :] * (2 * np.pi / n2)
    f2 = np.concatenate([np.cos(k2), -np.sin(k2)], axis=0)
    k1 = np.arange(n1)[:, None] * np.arange(n1)[None, :] * (2 * np.pi / n1)
    c1, s1 = np.cos(k1), np.sin(k1)
    m1 = np.block([[c1, s1], [-s1, c1]])
    tw = np.arange(n2)[:, None] * np.arange(n1)[None, :] * (2 * np.pi / seq)
    twc = np.broadcast_to(np.cos(tw)[:, :, None], (n2, n1, LANES))
    tws = np.broadcast_to(np.sin(tw)[:, :, None], (n2, n1, LANES))
    kc = np.arange(gc)[:, None] * np.arange(gc)[None, :] * (2 * np.pi / gc)
    mc = np.concatenate([np.cos(kc), np.sin(kc)], axis=0) / np.sqrt(seq * gc)
    return (jnp.asarray(f2, BF16), jnp.asarray(m1, BF16), jnp.asarray(twc, F32),
            jnp.asarray(tws, F32), jnp.asarray(mc, BF16))


def _fft1_kernel(f2_ref, x_ref, a_ref):
    a_ref[0] = jnp.dot(f2_ref[...], x_ref[0], preferred_element_type=F32).astype(BF16)


def _fft2_kernel(a_ref, twc_ref, tws_ref, m1_ref, mc_ref, wo_ref, y_ref, *, kb, gc, d):
    n1 = m1_ref.shape[0] // 2
    mw = a_ref.shape[-1]
    for j in range(kb):
        ar = a_ref[0, 0, j].astype(F32)
        ai = a_ref[0, 1, j].astype(F32)
        tc = jnp.tile(twc_ref[j], (1, mw // LANES))
        ts = jnp.tile(tws_ref[j], (1, mw // LANES))
        br = ar * tc + ai * ts
        bi = ai * tc - ar * ts
        bb = jnp.concatenate([br, bi], axis=0).astype(BF16)
        xx = jnp.dot(m1_ref[...], bb, preferred_element_type=F32)
        xr = xx[0:n1].astype(BF16)
        xi = xx[n1:2 * n1].astype(BF16)
        fs = []
        for g in range(mw // gc):
            lhs = jnp.concatenate([xr[:, g * gc:(g + 1) * gc], xi[:, g * gc:(g + 1) * gc]], axis=1)
            fs.append(jnp.dot(lhs, mc_ref[...], preferred_element_type=F32))
        f = jnp.concatenate(fs, axis=1).astype(BF16)
        y_ref[0, :, j * d:(j + 1) * d] = jnp.dot(
            f, wo_ref[...], preferred_element_type=F32).astype(BF16)


def _fourier(ud, tables, w_out, nb, seq, gc):
    f2, m1, twc, tws, mc = tables
    mw = ud.shape[1]
    d = w_out.shape[1]
    n1, n2 = FFT_N1, seq // FFT_N1
    xv = ud.reshape(nb, n2, n1 * mw)
    tn = 8192
    a = pl.pallas_call(
        _fft1_kernel,
        grid=(nb, n1 * mw // tn),
        in_specs=[
            pl.BlockSpec((2 * n2, n2), lambda b, j: (0, 0)),
            pl.BlockSpec((1, n2, tn), lambda b, j: (b, 0, j)),
        ],
        out_specs=pl.BlockSpec((1, 2 * n2, tn), lambda b, j: (b, 0, j)),
        out_shape=jax.ShapeDtypeStruct((nb, 2 * n2, n1 * mw), BF16),
        compiler_params=_cparams("arbitrary", "arbitrary"),
        name="fft_stage1",
    )(f2, xv)
    a5 = a.reshape(nb, 2, n2, n1, mw)
    kb = 8
    y = pl.pallas_call(
        functools.partial(_fft2_kernel, kb=kb, gc=gc, d=d),
        grid=(nb, n2 // kb),
        in_specs=[
            pl.BlockSpec((1, 2, kb, n1, mw), lambda b, k: (b, 0, k, 0, 0)),
            pl.BlockSpec((kb, n1, LANES), lambda b, k: (k, 0, 0)),
            pl.BlockSpec((kb, n1, LANES), lambda b, k: (k, 0, 0)),
            pl.BlockSpec(m1.shape, lambda b, k: (0, 0)),
            pl.BlockSpec(mc.shape, lambda b, k: (0, 0)),
            pl.BlockSpec(w_out.shape, lambda b, k: (0, 0)),
        ],
        out_specs=pl.BlockSpec((1, n1, kb * d), lambda b, k: (b, 0, k)),
        out_shape=jax.ShapeDtypeStruct((nb, n1, n2 * d), BF16),
        compiler_params=_cparams("arbitrary", "arbitrary"),
        name="fft_stage2",
    )(a5, twc, tws, m1, mc, w_out)
    return y.reshape(nb * seq, d)


def _merge_kernel(x_ref, g_ref, sh_ref, sc_ref, gt_ref, ya_ref, yb_ref, yc_ref, yd_ref,
                  wa_ref, wb_ref, wc_ref, wd_ref, wo_ref, o_ref, h_ref, acc_ref):
    j = pl.program_id(1)

    @pl.when(j == 0)
    def _():
        h_ref[...] = _norm_modulate(x_ref[...], g_ref[...], sh_ref[0], sc_ref[0]).astype(BF16)
        acc_ref[...] = jnp.zeros_like(acc_ref)

    h = h_ref[...]
    merged = None
    for w_ref, y_ref in ((wa_ref, ya_ref), (wb_ref, yb_ref), (wc_ref, yc_ref), (wd_ref, yd_ref)):
        gl = jnp.dot(h, w_ref[...], preferred_element_type=F32)
        term = _sigmoid(gl) * y_ref[...].astype(F32)
        merged = term if merged is None else merged + term
    acc_ref[...] += jnp.dot(merged.astype(BF16), wo_ref[...], preferred_element_type=F32)

    @pl.when(j == pl.num_programs(1) - 1)
    def _():
        o_ref[...] = x_ref[...] + gt_ref[0] * acc_ref[...]


def _merge(x2, g, sh, sc, gt, ys, w_in_b, off_g, w_o, seq):
    t, d = x2.shape
    tm, tc = 512, 512
    tps = seq // tm
    nj = d // tc
    gb = off_g // tc
    mod_spec = pl.BlockSpec((1, 1, d), lambda i, j: (i // tps, 0, 0))
    y_spec = pl.BlockSpec((tm, tc), lambda i, j: (i, j))
    w_specs = [pl.BlockSpec((d, tc), functools.partial(lambda i, j, k: (0, gb + k * nj + j), k=k))
               for k in range(N_BRANCH)]
    return pl.pallas_call(
        _merge_kernel,
        grid=(t // tm, nj),
        in_specs=[
            pl.BlockSpec((tm, d), lambda i, j: (i, 0)),
            pl.BlockSpec((1, d), lambda i, j: (0, 0)),
            mod_spec, mod_spec, mod_spec,
            y_spec, y_spec, y_spec, y_spec,
            *w_specs,
            pl.BlockSpec((tc, d), lambda i, j: (j, 0)),
        ],
        out_specs=pl.BlockSpec((tm, d), lambda i, j: (i, 0)),
        out_shape=jax.ShapeDtypeStruct((t, d), F32),
        scratch_shapes=[pltpu.VMEM((tm, d), BF16), pltpu.VMEM((tm, d), F32)],
        compiler_params=_cparams("arbitrary", "arbitrary"),
        name="merge",
    )(x2, g, sh, sc, gt, *ys, w_in_b, w_in_b, w_in_b, w_in_b, w_o)


def _pack_rows(v):
    half = v.shape[1] // 2
    hi = lax.bitcast_convert_type(v[:, :half].astype(BF16).astype(F32), U32)
    lo = lax.bitcast_convert_type(v[:, half:].astype(BF16).astype(F32), U32)
    return hi | (lo >> 16)


def _unpack_rows(p):
    hi = lax.bitcast_convert_type(p & jnp.uint32(0xFFFF0000), F32)
    lo = lax.bitcast_convert_type(p << 16, F32)
    return hi, lo


def _router_kernel(x_ref, g_ref, sh_ref, sc_ref, whi_ref, wlo_ref, b_ref,
                   hp_ref, ri_ref, rf_ref, cnt_ref, base_ref, *, tm):
    i = pl.program_id(0)

    @pl.when(i == 0)
    def _():
        base_ref[...] = jnp.zeros_like(base_ref)

    h = _norm_modulate(x_ref[...], g_ref[...], sh_ref[0], sc_ref[0])
    hp_ref[...] = _pack_rows(h)
    h_hi = h.astype(BF16)
    h_lo = (h - h_hi.astype(F32)).astype(BF16)
    whi = whi_ref[...]
    logits = (jnp.dot(h_hi, whi, preferred_element_type=F32)
              + jnp.dot(h_lo, whi, preferred_element_type=F32)
              + jnp.dot(h_hi, wlo_ref[...], preferred_element_type=F32)) + b_ref[...]

    lane = lax.broadcasted_iota(I32, (tm, LANES), 1)
    lane_f = lane.astype(F32)
    is_grp = (lane >= N_EXPERTS) & (lane < N_EXPERTS + N_GROUPS)
    lg = jnp.where(is_grp, logits, NEG)
    mg = jnp.max(lg, axis=1, keepdims=True)
    gidx = jnp.min(jnp.where(lg == mg, lane_f - N_EXPERTS, 1e9), axis=1, keepdims=True)
    pg_top = 1.0 / jnp.sum(jnp.where(is_grp, jnp.exp(lg - mg), 0.0), axis=1, keepdims=True)

    in_grp = (lane < N_EXPERTS) & (jnp.right_shift(lane, 3).astype(F32) == gidx)
    le = jnp.where(in_grp, logits, NEG)
    m1 = jnp.max(le, axis=1, keepdims=True)
    e1 = jnp.min(jnp.where(le == m1, lane_f, 1e9), axis=1, keepdims=True)
    le2 = jnp.where(lane_f == e1, NEG, le)
    m2 = jnp.max(le2, axis=1, keepdims=True)
    e2 = jnp.min(jnp.where(le2 == m2, lane_f, 1e9), axis=1, keepdims=True)
    w1 = 1.0 / (1.0 + jnp.exp(m2 - m1))
    w2 = 1.0 - w1

    is1 = lane_f == e1
    is2 = lane_f == e2
    oh = jnp.where(is1 | is2, 1.0, 0.0)
    r_io = lax.broadcasted_iota(I32, (tm, tm), 0)
    c_io = lax.broadcasted_iota(I32, (tm, tm), 1)
    tri = jnp.where(r_io > c_io, 1.0, 0.0).astype(BF16)
    prior = jnp.dot(tri, oh.astype(BF16), preferred_element_type=F32) + base_ref[0:1, :]
    pos1 = jnp.sum(jnp.where(is1, prior, 0.0), axis=1, keepdims=True)
    pos2 = jnp.sum(jnp.where(is2, prior, 0.0), axis=1, keepdims=True)
    base_ref[0:1, :] = base_ref[0:1, :] + jnp.sum(oh, axis=0, keepdims=True)

    ints = jnp.where(lane == 0, e1, jnp.where(lane == 1, e2, jnp.where(lane == 2, pos1, pos2)))
    ri_ref[...] = ints.astype(I32)
    rf_ref[...] = jnp.where(lane == 0, w1 * pg_top, w2 * pg_top)
    cnt_ref[...] = base_ref[...]


def _router(x2, g, sh, sc, w_hi, w_lo, bias, seq):
    t, d = x2.shape
    tm = 512
    tps = seq // tm
    mod_spec = pl.BlockSpec((1, 1, d), lambda i: (i // tps, 0, 0))
    row_spec = pl.BlockSpec((tm, LANES), lambda i: (i, 0))
    return pl.pallas_call(
        functools.partial(_router_kernel, tm=tm),
        grid=(t // tm,),
        in_specs=[
            pl.BlockSpec((tm, d), lambda i: (i, 0)),
            pl.BlockSpec((1, d), lambda i: (0, 0)),
            mod_spec, mod_spec,
            pl.BlockSpec((d, LANES), lambda i: (0, 0)),
            pl.BlockSpec((d, LANES), lambda i: (0, 0)),
            pl.BlockSpec((1, LANES), lambda i: (0, 0)),
        ],
        out_specs=[pl.BlockSpec((tm, d // 2), lambda i: (i, 0)), row_spec, row_spec,
                   pl.BlockSpec((8, LANES), lambda i: (0, 0))],
        out_shape=[jax.ShapeDtypeStruct((t, d // 2), U32),
                   jax.ShapeDtypeStruct((t, LANES), I32),
                   jax.ShapeDtypeStruct((t, LANES), F32),
                   jax.ShapeDtypeStruct((8, LANES), F32)],
        scratch_shapes=[pltpu.VMEM((8, LANES), F32)],
        compiler_params=_cparams("arbitrary"),
        name="router",
    )(x2, g, sh, sc, w_hi, w_lo, bias)


def _dispatch_kernel(d1_ref, d2_ref, h_ref, xb_in_ref, xb_ref, sem, *, tm):
    del xb_in_ref
    base = pl.program_id(0) * tm

    def row_copy(r, dst):
        return pltpu.make_async_copy(h_ref.at[pl.ds(r, 1), :], xb_ref.at[pl.ds(dst, 1), :], sem)

    def issue(r, carry):
        row_copy(r, d1_ref[base + r]).start()
        row_copy(r, d2_ref[base + r]).start()
        return carry

    lax.fori_loop(0, tm, issue, 0)

    def drain(r, carry):
        row_copy(r, 0).wait()
        row_copy(r, 0).wait()
        return carry

    lax.fori_loop(0, tm, drain, 0)


def _dispatch(dest1, dest2, hp, n_rows):
    t, hw = hp.shape
    tm = 512
    xb0 = jnp.zeros((n_rows, hw), U32)
    return pl.pallas_call(
        functools.partial(_dispatch_kernel, tm=tm),
        grid_spec=pltpu.PrefetchScalarGridSpec(
            num_scalar_prefetch=2,
            grid=(t // tm,),
            in_specs=[pl.BlockSpec((tm, hw), lambda i, d1, d2: (i, 0)),
                      pl.BlockSpec(memory_space=pl.ANY)],
            out_specs=pl.BlockSpec(memory_space=pl.ANY),
            scratch_shapes=[pltpu.SemaphoreType.DMA(())],
        ),
        out_shape=jax.ShapeDtypeStruct((n_rows, hw), U32),
        input_output_aliases={3: 0},
        compiler_params=_cparams("arbitrary"),
        name="dispatch",
    )(dest1, dest2, hp, xb0)


def _experts_kernel(be_ref, nu_ref, xb_ref, w1_ref, w3_ref, w2_ref, yb_ref, w1b, w3b, w2b):
    b = pl.program_id(0)
    prev = be_ref[jnp.maximum(b - 1, 0)]

    @pl.when((b == 0) | (be_ref[b] != prev))
    def _():
        w1b[...] = w1_ref[0].astype(BF16)
        w3b[...] = w3_ref[0].astype(BF16)
        w2b[...] = w2_ref[0].astype(BF16)

    @pl.when(b < nu_ref[0])
    def _():
        hi, lo = _unpack_rows(xb_ref[...])
        xrow = jnp.concatenate([hi.astype(BF16), lo.astype(BF16)], axis=1)
        h1 = jnp.dot(xrow, w1b[...], preferred_element_type=F32)
        h3 = jnp.dot(xrow, w3b[...], preferred_element_type=F32)
        hh = (h1 * _sigmoid(h1) * h3).astype(BF16)
        yb_ref[...] = _pack_rows(jnp.dot(hh, w2b[...], preferred_element_type=F32))

    @pl.when(b >= nu_ref[0])
    def _():
        yb_ref[...] = jnp.zeros_like(yb_ref)


def _experts(block_exp, n_used, xb, w1, w3, w2):
    n_rows, hw = xb.shape
    _, d, de = w1.shape
    bm = EXPERT_ROWS
    return pl.pallas_call(
        _experts_kernel,
        grid_spec=pltpu.PrefetchScalarGridSpec(
            num_scalar_prefetch=2,
            grid=(n_rows // bm,),
            in_specs=[
                pl.BlockSpec((bm, hw), lambda b, be, nu: (b, 0)),
                pl.BlockSpec((1, d, de), lambda b, be, nu: (be[b], 0, 0)),
                pl.BlockSpec((1, d, de), lambda b, be, nu: (be[b], 0, 0)),
                pl.BlockSpec((1, de, d), lambda b, be, nu: (be[b], 0, 0)),
            ],
            out_specs=pl.BlockSpec((bm, hw), lambda b, be, nu: (b, 0)),
            scratch_shapes=[pltpu.VMEM((d, de), BF16), pltpu.VMEM((d, de), BF16),
                            pltpu.VMEM((de, d), BF16)],
        ),
        out_shape=jax.ShapeDtypeStruct((n_rows, hw), U32),
        compiler_params=_cparams("arbitrary"),
        name="experts",
    )(block_exp, n_used, xb, w1, w3, w2)


def _combine_kernel(d1_ref, d2_ref, x_ref, rf_ref, gt_ref, fg_ref, yb_ref, o_ref,
                    r1_ref, r2_ref, sem, *, tm, final):
    base = pl.program_id(0) * tm

    def row_copy(src, buf, r):
        return pltpu.make_async_copy(yb_ref.at[pl.ds(src, 1), :], buf.at[pl.ds(r, 1), :], sem)

    def issue(r, carry):
        row_copy(d1_ref[base + r], r1_ref, r).start()
        row_copy(d2_ref[base + r], r2_ref, r).start()
        return carry

    lax.fori_loop(0, tm, issue, 0)

    def drain(r, carry):
        row_copy(0, r1_ref, r).wait()
        row_copy(0, r2_ref, r).wait()
        return carry

    lax.fori_loop(0, tm, drain, 0)

    half = x_ref.shape[1] // 2
    w1 = rf_ref[:, 0:1]
    w2 = rf_ref[:, 1:2]
    a_hi, a_lo = _unpack_rows(r1_ref[...])
    b_hi, b_lo = _unpack_rows(r2_ref[...])
    gt = gt_ref[0]
    xn_hi = x_ref[:, :half] + gt[:, :half] * (a_hi * w1 + b_hi * w2)
    xn_lo = x_ref[:, half:] + gt[:, half:] * (a_lo * w1 + b_lo * w2)
    if final:
        ms = (jnp.sum(xn_hi * xn_hi, axis=-1, keepdims=True)
              + jnp.sum(xn_lo * xn_lo, axis=-1, keepdims=True)) / x_ref.shape[1]
        r = lax.rsqrt(ms + EPS)
        xn_hi = xn_hi * r * fg_ref[:, :half]
        xn_lo = xn_lo * r * fg_ref[:, half:]
    o_ref[:, :half] = xn_hi
    o_ref[:, half:] = xn_lo


def _combine(dest1, dest2, x2, rf, gt, final_g, yb, seq, final):
    t, d = x2.shape
    hw = yb.shape[1]
    tm = 512
    tps = seq // tm
    return pl.pallas_call(
        functools.partial(_combine_kernel, tm=tm, final=final),
        grid_spec=pltpu.PrefetchScalarGridSpec(
            num_scalar_prefetch=2,
            grid=(t // tm,),
            in_specs=[
                pl.BlockSpec((tm, d), lambda i, d1, d2: (i, 0)),
                pl.BlockSpec((tm, LANES), lambda i, d1, d2: (i, 0)),
                pl.BlockSpec((1, 1, d), lambda i, d1, d2: (i // tps, 0, 0)),
                pl.BlockSpec((1, d), lambda i, d1, d2: (0, 0)),
                pl.BlockSpec(memory_space=pl.ANY),
            ],
            out_specs=pl.BlockSpec((tm, d), lambda i, d1, d2: (i, 0)),
            scratch_shapes=[pltpu.VMEM((tm, hw), U32), pltpu.VMEM((tm, hw), U32),
                            pltpu.SemaphoreType.DMA(())],
        ),
        out_shape=jax.ShapeDtypeStruct((t, d), F32),
        compiler_params=_cparams("arbitrary"),
        name="combine",
    )(dest1, dest2, x2, rf, gt, final_g.reshape(1, d), yb)


def _routing_tables(ri, cnt, n_blocks):
    bm = EXPERT_ROWS
    counts = cnt[0, :N_EXPERTS].astype(I32)
    pcounts = (counts + bm - 1) // bm * bm
    pends = jnp.cumsum(pcounts)
    pstarts = pends - pcounts
    dest1 = pstarts[ri[:, 0]] + ri[:, 2]
    dest2 = pstarts[ri[:, 1]] + ri[:, 3]
    block_exp = jnp.clip(
        jnp.searchsorted(pends, jnp.arange(n_blocks, dtype=I32) * bm, side="right"),
        0, N_EXPERTS - 1).astype(I32)
    n_used = (pends[-1:] // bm).astype(I32)
    return dest1.astype(I32), dest2.astype(I32), block_exp, n_used


def kernel(x, c, ada_w, ada_b, norm1_g, w_in, pool_w, pool_scale, conf_dw_w, conf_dw_b,
           conf_ln_g, conf_ln_b, conf_w_out, sc_conv_w, sc_w_out, fnet_w_out, w_o, norm2_g,
           rg_w, rg_b, re_w, re_b, exp_w1, exp_w3, exp_w2, final_g):
    nb, seq, d = x.shape
    depth = ada_w.shape[0]
    t = nb * seq
    mw = d // N_BRANCH
    off_d = 6 * mw
    off_g = 7 * mw
    gc = mw // N_GROUPS
    n_rows = t * 2 + N_EXPERTS * EXPERT_ROWS
    n_blocks = n_rows // EXPERT_ROWS

    mod = _ada_mod(c, ada_w, ada_b)
    band = jnp.asarray(_pool_band(512), BF16)
    tables = _dft_tables(seq, gc)
    x2 = x.reshape(t, d)

    for l in range(depth):
        sh1, sc1, gt1, sh2, sc2, gt2 = [mod[l, :, k * d:(k + 1) * d].reshape(nb, 1, d)
                                        for k in range(6)]
        w_in_b = w_in[l].astype(BF16)
        g1 = norm1_g[l].reshape(1, d)
        u = _inproj(x2, g1, sh1, sc1, w_in_b, off_g, seq)
        ya, yb, yc = _mixers(u, band, pool_w[l].astype(BF16), pool_scale[l], conf_dw_w[l],
                             conf_dw_b[l], conf_ln_g[l], conf_ln_b[l],
                             conf_w_out[l].astype(BF16), sc_conv_w[l],
                             sc_w_out[l].astype(BF16), seq, d)
        yd = _fourier(u[:, off_d:off_g], tables, fnet_w_out[l].astype(BF16), nb, seq, gc)
        x2 = _merge(x2, g1, sh1, sc1, gt1, (ya, yb, yc, yd), w_in_b, off_g,
                    w_o[l].astype(BF16), seq)

        wr = jnp.zeros((d, LANES), F32).at[:, :N_EXPERTS].set(re_w[l])
        wr = wr.at[:, N_EXPERTS:N_EXPERTS + N_GROUPS].set(rg_w[l])
        br = jnp.zeros((1, LANES), F32).at[0, :N_EXPERTS].set(re_b[l])
        br = br.at[0, N_EXPERTS:N_EXPERTS + N_GROUPS].set(rg_b[l])
        wr_hi = wr.astype(BF16)
        wr_lo = (wr - wr_hi.astype(F32)).astype(BF16)
        hp, ri, rf, cnt = _router(x2, norm2_g[l].reshape(1, d), sh2, sc2, wr_hi, wr_lo, br, seq)
        dest1, dest2, block_exp, n_used = _routing_tables(ri, cnt, n_blocks)
        xb = _dispatch(dest1, dest2, hp, n_rows)
        ybk = _experts(block_exp, n_used, xb, exp_w1[l], exp_w3[l], exp_w2[l])
        x2 = _combine(dest1, dest2, x2, rf, gt2, final_g, ybk, seq, final=(l == depth - 1))
    return x2.reshape(nb, seq, d)
```

```python
import functools

import numpy as np
import jax
import jax.numpy as jnp
from jax import lax
from jax.experimental import pallas as pl
from jax.experimental.pallas import tpu as pltpu

F32 = jnp.float32
BF16 = jnp.bfloat16
U32 = jnp.uint32
I32 = jnp.int32

LANES = 128
EPS = 1e-6
N_BRANCH = 4
POOL_WINDOWS = (2, 4, 8, 16)
CONF_KERNEL = 31
SC_KERNEL = 3
N_GROUPS = 4
EXPERTS_PER_GROUP = 8
N_EXPERTS = N_GROUPS * EXPERTS_PER_GROUP
HALO = 64
EXPERT_ROWS = 256
FFT_N1 = 128
VMEM_LIMIT = 56 * 1024 * 1024
NEG = -1e30


def _cparams(*sem):
    return pltpu.CompilerParams(dimension_semantics=sem, vmem_limit_bytes=VMEM_LIMIT)


def _sigmoid(v):
    return 1.0 / (1.0 + jnp.exp(-v))


def _norm_modulate(x, g, sh, sc):
    ms = jnp.mean(x * x, axis=-1, keepdims=True)
    return (x * lax.rsqrt(ms + EPS)) * (g * (1.0 + sc)) + sh


def _ada_kernel(cb_ref, w_ref, b_ref, o_ref, *, nb, tn):
    w = w_ref[0]
    for b in range(nb):
        cbt = jnp.tile(cb_ref[b], (1, tn // LANES))
        o_ref[0, b:b + 1, :] = jnp.sum(w * cbt, axis=0, keepdims=True) + b_ref[0]


def _ada_mod(c, ada_w, ada_b):
    nl, d, n = ada_w.shape
    nb = c.shape[0]
    tn = 512
    cb = jnp.broadcast_to(c[:, :, None], (nb, d, LANES))
    return pl.pallas_call(
        functools.partial(_ada_kernel, nb=nb, tn=tn),
        grid=(nl, n // tn),
        in_specs=[
            pl.BlockSpec((nb, d, LANES), lambda l, j: (0, 0, 0)),
            pl.BlockSpec((1, d, tn), lambda l, j: (l, 0, j)),
            pl.BlockSpec((1, 1, tn), lambda l, j: (l, 0, j)),
        ],
        out_specs=pl.BlockSpec((1, nb, tn), lambda l, j: (l, 0, j)),
        out_shape=jax.ShapeDtypeStruct((nl, nb, n), F32),
        compiler_params=_cparams("arbitrary", "arbitrary"),
        name="ada_mod",
    )(cb, ada_w, ada_b.reshape(nl, 1, n))


def _inproj_kernel(x_ref, g_ref, sh_ref, sc_ref, w_ref, u_ref, h_ref):
    @pl.when(pl.program_id(1) == 0)
    def _():
        h_ref[...] = _norm_modulate(x_ref[...], g_ref[...], sh_ref[0], sc_ref[0]).astype(BF16)

    u_ref[...] = jnp.dot(h_ref[...], w_ref[0], preferred_element_type=F32).astype(BF16)


def _inproj(x2, g, sh, sc, w_in_b, layer, n, seq):
    t, d = x2.shape
    tm, tn = 1024, 512
    tps = seq // tm
    return pl.pallas_call(
        _inproj_kernel,
        grid=(t // tm, n // tn),
        in_specs=[
            pl.BlockSpec((tm, d), lambda i, j: (i, 0)),
            pl.BlockSpec((1, d), lambda i, j: (0, 0)),
            pl.BlockSpec((1, 1, d), lambda i, j: (i // tps, 0, 0)),
            pl.BlockSpec((1, 1, d), lambda i, j: (i // tps, 0, 0)),
            pl.BlockSpec((1, d, tn), lambda i, j: (layer, 0, j)),
        ],
        out_specs=pl.BlockSpec((tm, tn), lambda i, j: (i, j)),
        out_shape=jax.ShapeDtypeStruct((t, n), BF16),
        scratch_shapes=[pltpu.VMEM((tm, d), BF16)],
        compiler_params=_cparams("arbitrary", "arbitrary"),
        name="inproj",
    )(x2, g, sh, sc, w_in_b)


def _pool_band(tm):
    bands = np.zeros((len(POOL_WINDOWS), tm, tm + 2 * HALO), np.float32)
    for gi, w in enumerate(POOL_WINDOWS):
        left = w // 2
        for i in range(tm):
            lo = i + HALO - left
            bands[gi, i, lo:lo + w] = 1.0
    return bands


def _mixers_kernel(um_ref, up_ref, un_ref, band_ref, poolw_ref, pscale_ref,
                   dww_ref, dwb_ref, lng_ref, lnb_ref, cwo_ref, scw_ref, swo_ref,
                   ya_ref, yb_ref, yc_ref, ub_ref, vb_ref, pb_ref, cv_ref,
                   *, tm, tps, seq, mw):
    ti = pl.program_id(0) % tps
    rows = tm + 2 * HALO
    ub_ref[0:HALO, :] = jnp.where(ti > 0, up_ref[...], jnp.zeros_like(up_ref))
    ub_ref[HALO:HALO + tm, :] = um_ref[...]
    ub_ref[HALO + tm:rows, :] = jnp.where(ti < tps - 1, un_ref[...], jnp.zeros_like(un_ref))

    pos = (ti * tm + lax.broadcasted_iota(I32, (tm, 1), 0)).astype(F32)
    gc = mw // len(POOL_WINDOWS)
    dout = ya_ref.shape[1] // len(POOL_WINDOWS)
    for gi, w in enumerate(POOL_WINDOWS):
        left = w // 2
        right = w - 1 - left
        lo = jnp.clip(pos - left, 0.0, float(seq))
        hi = jnp.clip(pos + (right + 1), 0.0, float(seq))
        cnt = hi - lo
        ug = ub_ref[:, gi * gc:(gi + 1) * gc]
        wsum = jnp.dot(band_ref[gi], ug, preferred_element_type=F32)
        p = wsum / cnt - ug[HALO:HALO + tm, :].astype(F32)
        y = jnp.dot(p.astype(BF16), poolw_ref[gi], preferred_element_type=F32)
        ya_ref[:, gi * dout:(gi + 1) * dout] = (
            y * pscale_ref[:, gi * dout:(gi + 1) * dout]).astype(BF16)

    a = ub_ref[:, mw:2 * mw].astype(F32)
    gate = ub_ref[:, 2 * mw:3 * mw].astype(F32)
    vb_ref[...] = a * _sigmoid(gate)
    rc = 128
    for r0 in range(0, tm, rc):
        for c0 in range(0, mw, LANES):
            acc = jnp.zeros((rc, LANES), F32)
            for k in range(CONF_KERNEL):
                off = r0 + HALO - CONF_KERNEL // 2 + k
                acc = acc + dww_ref[k:k + 1, c0:c0 + LANES] * vb_ref[off:off + rc, c0:c0 + LANES]
            cv_ref[r0:r0 + rc, c0:c0 + LANES] = acc
    v = cv_ref[...] + dwb_ref[...]
    mu = jnp.mean(v, axis=-1, keepdims=True)
    dv = v - mu
    var = jnp.mean(dv * dv, axis=-1, keepdims=True)
    ln = dv * lax.rsqrt(var + EPS) * lng_ref[...] + lnb_ref[...]
    act = ln * _sigmoid(ln)
    yb_ref[...] = jnp.dot(act.astype(BF16), cwo_ref[...], preferred_element_type=F32).astype(BF16)

    pb_ref[...] = ub_ref[:, 4 * mw:5 * mw].astype(F32) * ub_ref[:, 5 * mw:6 * mw].astype(F32)
    z = jnp.zeros((tm, mw), F32)
    for k in range(SC_KERNEL):
        off = HALO - SC_KERNEL // 2 + k
        z = z + scw_ref[k:k + 1, :] * pb_ref[off:off + tm, :]
    bg = ub_ref[HALO:HALO + tm, 3 * mw:4 * mw].astype(F32)
    yc_ref[...] = jnp.dot((bg * z).astype(BF16), swo_ref[...], preferred_element_type=F32).astype(BF16)


def _mixers(u, band, pool_w, pool_scale, dw_w, dw_b, ln_g, ln_b, conf_wo, sc_w, sc_wo, seq, d):
    t = u.shape[0]
    mw = pool_w.shape[0] * pool_w.shape[1]
    tm = band.shape[1]
    tps = seq // tm
    hb = tm // HALO
    nhalo = t // HALO
    wabc = 6 * mw
    full = lambda shape: pl.BlockSpec(shape, lambda i: (0,) * len(shape))
    out_sds = jax.ShapeDtypeStruct((t, d), BF16)
    return pl.pallas_call(
        functools.partial(_mixers_kernel, tm=tm, tps=tps, seq=seq, mw=mw),
        grid=(t // tm,),
        in_specs=[
            pl.BlockSpec((tm, wabc), lambda i: (i, 0)),
            pl.BlockSpec((HALO, wabc), lambda i: (jnp.maximum(i * hb - 1, 0), 0)),
            pl.BlockSpec((HALO, wabc), lambda i: (jnp.minimum((i + 1) * hb, nhalo - 1), 0)),
            full(band.shape), full(pool_w.shape), full((1, d)),
            full(dw_w.shape), full((1, mw)), full((1, mw)), full((1, mw)),
            full(conf_wo.shape), full(sc_w.shape), full(sc_wo.shape),
        ],
        out_specs=[pl.BlockSpec((tm, d), lambda i: (i, 0))] * 3,
        out_shape=[out_sds] * 3,
        scratch_shapes=[
            pltpu.VMEM((tm + 2 * HALO, wabc), BF16),
            pltpu.VMEM((tm + 2 * HALO, mw), F32),
            pltpu.VMEM((tm + 2 * HALO, mw), F32),
            pltpu.VMEM((tm, mw), F32),
        ],
        compiler_params=_cparams("arbitrary"),
        name="mixers",
    )(u, u, u, band, pool_w, pool_scale.reshape(1, d), dw_w, dw_b.reshape(1, mw),
      ln_g.reshape(1, mw), ln_b.reshape(1, mw), conf_wo, sc_w, sc_wo)


def _dft_tables(seq, gc):
    n1, n2 = FFT_N1, seq // FFT_N1
    k2 = np.arange(n2)[:, None] * np.arange(n2)[None, :] * (2 * np.pi / n2)
    f2 = np.concatenate([np.cos(k2), -np.sin(k2)], axis=0)
    k1 = np.arange(n1)[:, None] * np.arange(n1)[None, :] * (2 * np.pi / n1)
    c1, s1 = np.cos(k1), np.sin(k1)
    m1 = np.block([[c1, s1], [-s1, c1]])
    tw = np.arange(n2)[:, None] * np.arange(n1)[None, :] * (2 * np.pi / seq)
    twc = np.broadcast_to(np.cos(tw)[:, :, None], (n2, n1, LANES))
    tws = np.broadcast_to(np.sin(tw)[:, :, None], (n2, n1, LANES))
    kc = np.arange(gc)[:, None] * np.arange(gc)[None, :] * (2 * np.pi / gc)
    mc = np.concatenate([np.cos(kc), np.sin(kc)], axis=0) / np.sqrt(seq * gc)
    return (jnp.asarray(f2, BF16), jnp.asarray(m1, BF16), jnp.asarray(twc, F32),
            jnp.asarray(tws, F32), jnp.asarray(mc, BF16))


def _fft1_kernel(f2_ref, x_ref, a_ref):
    a_ref[0] = jnp.dot(f2_ref[...], x_ref[0], preferred_element_type=F32).astype(BF16)


def _fft2_kernel(a_ref, twc_ref, tws_ref, m1_ref, mc_ref, wo_ref, y_ref, *, kb, gc, d):
    n1 = m1_ref.shape[0] // 2
    mw = a_ref.shape[-1]
    for j in range(kb):
        ar = a_ref[0, 0, j].astype(F32)
        ai = a_ref[0, 1, j].astype(F32)
        tc = jnp.tile(twc_ref[j], (1, mw // LANES))
        ts = jnp.tile(tws_ref[j], (1, mw // LANES))
        br = ar * tc + ai * ts
        bi = ai * tc - ar * ts
        bb = jnp.concatenate([br, bi], axis=0).astype(BF16)
        xx = jnp.dot(m1_ref[...], bb, preferred_element_type=F32)
        xr = xx[0:n1].astype(BF16)
        xi = xx[n1:2 * n1].astype(BF16)
        fs = []
        for g in range(mw // gc):
            lhs = jnp.concatenate([xr[:, g * gc:(g + 1) * gc], xi[:, g * gc:(g + 1) * gc]], axis=1)
            fs.append(jnp.dot(lhs, mc_ref[...], preferred_element_type=F32))
        f = jnp.concatenate(fs, axis=1).astype(BF16)
        y_ref[0, :, j * d:(j + 1) * d] = jnp.dot(
            f, wo_ref[...], preferred_element_type=F32).astype(BF16)


def _fourier(ud, tables, w_out, nb, seq, gc):
    f2, m1, twc, tws, mc = tables
    mw = ud.shape[1]
    d = w_out.shape[1]
    n1, n2 = FFT_N1, seq // FFT_N1
    xv = ud.reshape(nb, n2, n1 * mw)
    tn = 8192
    a = pl.pallas_call(
        _fft1_kernel,
        grid=(nb, n1 * mw // tn),
        in_specs=[
            pl.BlockSpec((2 * n2, n2), lambda b, j: (0, 0)),
            pl.BlockSpec((1, n2, tn), lambda b, j: (b, 0, j)),
        ],
        out_specs=pl.BlockSpec((1, 2 * n2, tn), lambda b, j: (b, 0, j)),
        out_shape=jax.ShapeDtypeStruct((nb, 2 * n2, n1 * mw), BF16),
        compiler_params=_cparams("arbitrary", "arbitrary"),
        name="fft_stage1",
    )(f2, xv)
    a5 = a.reshape(nb, 2, n2, n1, mw)
    kb = 8
    y = pl.pallas_call(
        functools.partial(_fft2_kernel, kb=kb, gc=gc, d=d),
        grid=(nb, n2 // kb),
        in_specs=[
            pl.BlockSpec((1, 2, kb, n1, mw), lambda b, k: (b, 0, k, 0, 0)),
            pl.BlockSpec((kb, n1, LANES), lambda b, k: (k, 0, 0)),
            pl.BlockSpec((kb, n1, LANES), lambda b, k: (k, 0, 0)),
            pl.BlockSpec(m1.shape, lambda b, k: (0, 0)),
            pl.BlockSpec(mc.shape, lambda b, k: (0, 0)),
            pl.BlockSpec(w_out.shape, lambda b, k: (0, 0)),
        ],
        out_specs=pl.BlockSpec((1, n1, kb * d), lambda b, k: (b, 0, k)),
        out_shape=jax.ShapeDtypeStruct((nb, n1, n2 * d), BF16),
        compiler_params=_cparams("arbitrary", "arbitrary"),
        name="fft_stage2",
    )(a5, twc, tws, m1, mc, w_out)
    return y.reshape(nb * seq, d)


def _merge_kernel(x_ref, g_ref, sh_ref, sc_ref, gt_ref, ya_ref, yb_ref, yc_ref, yd_ref,
                  wa_ref, wb_ref, wc_ref, wd_ref, wo_ref, o_ref, h_ref, acc_ref):
    j = pl.program_id(1)

    @pl.when(j == 0)
    def _():
        h_ref[...] = _norm_modulate(x_ref[...], g_ref[...], sh_ref[0], sc_ref[0]).astype(BF16)
        acc_ref[...] = jnp.zeros_like(acc_ref)

    h = h_ref[...]
    merged = None
    for w_ref, y_ref in ((wa_ref, ya_ref), (wb_ref, yb_ref), (wc_ref, yc_ref), (wd_ref, yd_ref)):
        gl = jnp.dot(h, w_ref[0], preferred_element_type=F32)
        term = _sigmoid(gl) * y_ref[...].astype(F32)
        merged = term if merged is None else merged + term
    acc_ref[...] += jnp.dot(merged.astype(BF16), wo_ref[0], preferred_element_type=F32)

    @pl.when(j == pl.num_programs(1) - 1)
    def _():
        o_ref[...] = x_ref[...] + gt_ref[0] * acc_ref[...]


def _merge(x2, g, sh, sc, gt, ys, w_in_b, layer, off_g, w_o_b, seq):
    t, d = x2.shape
    tm, tc = 512, 512
    tps = seq // tm
    nj = d // tc
    gb = off_g // tc
    mod_spec = pl.BlockSpec((1, 1, d), lambda i, j: (i // tps, 0, 0))
    y_spec = pl.BlockSpec((tm, tc), lambda i, j: (i, j))
    w_specs = [pl.BlockSpec((1, d, tc),
                            functools.partial(lambda i, j, k: (layer, 0, gb + k * nj + j), k=k))
               for k in range(N_BRANCH)]
    return pl.pallas_call(
        _merge_kernel,
        grid=(t // tm, nj),
        in_specs=[
            pl.BlockSpec((tm, d), lambda i, j: (i, 0)),
            pl.BlockSpec((1, d), lambda i, j: (0, 0)),
            mod_spec, mod_spec, mod_spec,
            y_spec, y_spec, y_spec, y_spec,
            *w_specs,
            pl.BlockSpec((1, tc, d), lambda i, j: (layer, j, 0)),
        ],
        out_specs=pl.BlockSpec((tm, d), lambda i, j: (i, 0)),
        out_shape=jax.ShapeDtypeStruct((t, d), F32),
        scratch_shapes=[pltpu.VMEM((tm, d), BF16), pltpu.VMEM((tm, d), F32)],
        compiler_params=_cparams("arbitrary", "arbitrary"),
        name="merge",
    )(x2, g, sh, sc, gt, *ys, w_in_b, w_in_b, w_in_b, w_in_b, w_o_b)


ROW_SUB = 8


def _store_rows(ref, v):
    m, d = v.shape
    half = d // 2
    for s in range(ROW_SUB):
        hi = lax.bitcast_convert_type(
            v[:, s * LANES:(s + 1) * LANES].astype(BF16).astype(F32), U32)
        lo = lax.bitcast_convert_type(
            v[:, half + s * LANES:half + (s + 1) * LANES].astype(BF16).astype(F32), U32)
        ref[pl.ds(s, m, stride=ROW_SUB), :] = hi | (lo >> 16)


def _load_rows(ref, m, s):
    p = ref[pl.ds(s, m, stride=ROW_SUB), :]
    hi = lax.bitcast_convert_type(p & jnp.uint32(0xFFFF0000), F32)
    lo = lax.bitcast_convert_type(p << 16, F32)
    return hi, lo


def _router_kernel(x_ref, g_ref, sh_ref, sc_ref, whi_ref, wlo_ref, b_ref,
                   hp_ref, meta_ref, rf_ref, cnt_ref, base_ref, *, tm):
    i = pl.program_id(0)

    @pl.when(i == 0)
    def _():
        base_ref[...] = jnp.zeros_like(base_ref)

    h = _norm_modulate(x_ref[...], g_ref[...], sh_ref[0], sc_ref[0])
    _store_rows(hp_ref, h)
    h_hi = h.astype(BF16)
    h_lo = (h - h_hi.astype(F32)).astype(BF16)
    whi = whi_ref[...]
    logits = (jnp.dot(h_hi, whi, preferred_element_type=F32)
              + jnp.dot(h_lo, whi, preferred_element_type=F32)
              + jnp.dot(h_hi, wlo_ref[...], preferred_element_type=F32)) + b_ref[...]

    lane = lax.broadcasted_iota(I32, (tm, LANES), 1)
    lane_f = lane.astype(F32)
    is_grp = (lane >= N_EXPERTS) & (lane < N_EXPERTS + N_GROUPS)
    lg = jnp.where(is_grp, logits, NEG)
    mg = jnp.max(lg, axis=1, keepdims=True)
    gidx = jnp.min(jnp.where(lg == mg, lane_f - N_EXPERTS, 1e9), axis=1, keepdims=True)
    pg_top = 1.0 / jnp.sum(jnp.where(is_grp, jnp.exp(lg - mg), 0.0), axis=1, keepdims=True)

    in_grp = (lane < N_EXPERTS) & (jnp.right_shift(lane, 3).astype(F32) == gidx)
    le = jnp.where(in_grp, logits, NEG)
    m1 = jnp.max(le, axis=1, keepdims=True)
    e1 = jnp.min(jnp.where(le == m1, lane_f, 1e9), axis=1, keepdims=True)
    le2 = jnp.where(lane_f == e1, NEG, le)
    m2 = jnp.max(le2, axis=1, keepdims=True)
    e2 = jnp.min(jnp.where(le2 == m2, lane_f, 1e9), axis=1, keepdims=True)
    w1 = 1.0 / (1.0 + jnp.exp(m2 - m1))
    w2 = 1.0 - w1

    is1 = lane_f == e1
    is2 = lane_f == e2
    oh = jnp.where(is1 | is2, 1.0, 0.0)
    r_io = lax.broadcasted_iota(I32, (tm, tm), 0)
    c_io = lax.broadcasted_iota(I32, (tm, tm), 1)
    tri = jnp.where(r_io > c_io, 1.0, 0.0).astype(BF16)
    prior = jnp.dot(tri, oh.astype(BF16), preferred_element_type=F32) + base_ref[0:1, :]
    pos1 = jnp.sum(jnp.where(is1, prior, 0.0), axis=1, keepdims=True)
    pos2 = jnp.sum(jnp.where(is2, prior, 0.0), axis=1, keepdims=True)
    base_ref[0:1, :] = base_ref[0:1, :] + jnp.sum(oh, axis=0, keepdims=True)

    ints = jnp.where(lane == 0, e1, jnp.where(lane == 1, e2, jnp.where(lane == 2, pos1, pos2)))
    meta_ref[...] = jnp.transpose(ints)[0:8, :]
    rf_ref[...] = jnp.where(lane == 0, w1 * pg_top, w2 * pg_top)
    cnt_ref[...] = base_ref[...]


def _router(x2, g, sh, sc, w_hi, w_lo, bias, seq):
    t, d = x2.shape
    tm = 512
    tps = seq // tm
    mod_spec = pl.BlockSpec((1, 1, d), lambda i: (i // tps, 0, 0))
    row_spec = pl.BlockSpec((tm, LANES), lambda i: (i, 0))
    return pl.pallas_call(
        functools.partial(_router_kernel, tm=tm),
        grid=(t // tm,),
        in_specs=[
            pl.BlockSpec((tm, d), lambda i: (i, 0)),
            pl.BlockSpec((1, d), lambda i: (0, 0)),
            mod_spec, mod_spec,
            pl.BlockSpec((d, LANES), lambda i: (0, 0)),
            pl.BlockSpec((d, LANES), lambda i: (0, 0)),
            pl.BlockSpec((1, LANES), lambda i: (0, 0)),
        ],
        out_specs=[pl.BlockSpec((tm * ROW_SUB, LANES), lambda i: (i, 0)),
                   pl.BlockSpec((8, tm), lambda i: (0, i)),
                   row_spec,
                   pl.BlockSpec((8, LANES), lambda i: (0, 0))],
        out_shape=[jax.ShapeDtypeStruct((t * ROW_SUB, LANES), U32),
                   jax.ShapeDtypeStruct((8, t), F32),
                   jax.ShapeDtypeStruct((t, LANES), F32),
                   jax.ShapeDtypeStruct((8, LANES), F32)],
        scratch_shapes=[pltpu.VMEM((8, LANES), F32)],
        compiler_params=_cparams("arbitrary"),
        name="router",
    )(x2, g, sh, sc, w_hi, w_lo, bias)


def _dispatch_kernel(d1_ref, d2_ref, h_ref, xb_in_ref, xb_ref, sem, *, tm):
    del xb_in_ref
    base = pl.program_id(0) * tm

    def row_copy(r, dst):
        return pltpu.make_async_copy(
            h_ref.at[pl.ds(pl.multiple_of(r * ROW_SUB, ROW_SUB), ROW_SUB), :],
            xb_ref.at[pl.ds(pl.multiple_of(dst * ROW_SUB, ROW_SUB), ROW_SUB), :], sem)

    def issue(r, carry):
        row_copy(r, d1_ref[base + r]).start()
        row_copy(r, d2_ref[base + r]).start()
        return carry

    lax.fori_loop(0, tm, issue, 0, unroll=8)

    def drain(r, carry):
        row_copy(r, 0).wait()
        row_copy(r, 0).wait()
        return carry

    lax.fori_loop(0, tm, drain, 0, unroll=8)


def _dispatch(dest1, dest2, hp, n_rows):
    t = hp.shape[0] // ROW_SUB
    tm = 512
    xb0 = jnp.zeros((n_rows * ROW_SUB, LANES), U32)
    return pl.pallas_call(
        functools.partial(_dispatch_kernel, tm=tm),
        grid_spec=pltpu.PrefetchScalarGridSpec(
            num_scalar_prefetch=2,
            grid=(t // tm,),
            in_specs=[pl.BlockSpec((tm * ROW_SUB, LANES), lambda i, d1, d2: (i, 0)),
                      pl.BlockSpec(memory_space=pl.ANY)],
            out_specs=pl.BlockSpec(memory_space=pl.ANY),
            scratch_shapes=[pltpu.SemaphoreType.DMA(())],
        ),
        out_shape=jax.ShapeDtypeStruct((n_rows * ROW_SUB, LANES), U32),
        input_output_aliases={3: 0},
        compiler_params=_cparams("arbitrary"),
        name="dispatch",
    )(dest1, dest2, hp, xb0)


def _experts_kernel(be_ref, nu_ref, xb_ref, w1_ref, w3_ref, w2_ref, yb_ref, w1b, w3b, w2b):
    b = pl.program_id(0)
    prev = be_ref[jnp.maximum(b - 1, 0)]

    @pl.when((b == 0) | (be_ref[b] != prev))
    def _():
        w1b[...] = w1_ref[0, 0].astype(BF16)
        w3b[...] = w3_ref[0, 0].astype(BF16)
        w2b[...] = w2_ref[0, 0].astype(BF16)

    @pl.when(b < nu_ref[0])
    def _():
        bm = xb_ref.shape[0] // ROW_SUB
        parts = [_load_rows(xb_ref, bm, s) for s in range(ROW_SUB)]
        xrow = jnp.concatenate([p[0].astype(BF16) for p in parts]
                               + [p[1].astype(BF16) for p in parts], axis=1)
        h1 = jnp.dot(xrow, w1b[...], preferred_element_type=F32)
        h3 = jnp.dot(xrow, w3b[...], preferred_element_type=F32)
        hh = (h1 * _sigmoid(h1) * h3).astype(BF16)
        _store_rows(yb_ref, jnp.dot(hh, w2b[...], preferred_element_type=F32))

    @pl.when(b >= nu_ref[0])
    def _():
        yb_ref[...] = jnp.zeros_like(yb_ref)


def _experts(block_exp, n_used, xb, w1, w3, w2, layer):
    n_rows = xb.shape[0] // ROW_SUB
    _, _, d, de = w1.shape
    bm = EXPERT_ROWS
    return pl.pallas_call(
        _experts_kernel,
        grid_spec=pltpu.PrefetchScalarGridSpec(
            num_scalar_prefetch=2,
            grid=(n_rows // bm,),
            in_specs=[
                pl.BlockSpec((bm * ROW_SUB, LANES), lambda b, be, nu: (b, 0)),
                pl.BlockSpec((1, 1, d, de), lambda b, be, nu: (layer, be[b], 0, 0)),
                pl.BlockSpec((1, 1, d, de), lambda b, be, nu: (layer, be[b], 0, 0)),
                pl.BlockSpec((1, 1, de, d), lambda b, be, nu: (layer, be[b], 0, 0)),
            ],
            out_specs=pl.BlockSpec((bm * ROW_SUB, LANES), lambda b, be, nu: (b, 0)),
            scratch_shapes=[pltpu.VMEM((d, de), BF16), pltpu.VMEM((d, de), BF16),
                            pltpu.VMEM((de, d), BF16)],
        ),
        out_shape=jax.ShapeDtypeStruct((n_rows * ROW_SUB, LANES), U32),
        compiler_params=_cparams("arbitrary"),
        name="experts",
    )(block_exp, n_used, xb, w1, w3, w2)


def _combine_kernel(d1_ref, d2_ref, x_ref, rf_ref, gt_ref, fg_ref, yb_ref, o_ref,
                    r1_ref, r2_ref, sem, *, tm, final):
    base = pl.program_id(0) * tm

    def row_copy(src, buf, r):
        return pltpu.make_async_copy(
            yb_ref.at[pl.ds(pl.multiple_of(src * ROW_SUB, ROW_SUB), ROW_SUB), :],
            buf.at[pl.ds(pl.multiple_of(r * ROW_SUB, ROW_SUB), ROW_SUB), :], sem)

    def issue(r, carry):
        row_copy(d1_ref[base + r], r1_ref, r).start()
        row_copy(d2_ref[base + r], r2_ref, r).start()
        return carry

    lax.fori_loop(0, tm, issue, 0, unroll=8)

    def drain(r, carry):
        row_copy(0, r1_ref, r).wait()
        row_copy(0, r2_ref, r).wait()
        return carry

    lax.fori_loop(0, tm, drain, 0, unroll=8)

    d = x_ref.shape[1]
    half = d // 2
    w1 = rf_ref[:, 0:1]
    w2 = rf_ref[:, 1:2]
    ms = jnp.zeros((tm, 1), F32)
    for s in range(ROW_SUB):
        a_hi, a_lo = _load_rows(r1_ref, tm, s)
        b_hi, b_lo = _load_rows(r2_ref, tm, s)
        for c0, ya, yb2 in ((s * LANES, a_hi, b_hi), (half + s * LANES, a_lo, b_lo)):
            xn = x_ref[:, c0:c0 + LANES] + gt_ref[0, :, c0:c0 + LANES] * (ya * w1 + yb2 * w2)
            o_ref[:, c0:c0 + LANES] = xn
            if final:
                ms = ms + jnp.sum(xn * xn, axis=-1, keepdims=True)
    if final:
        o_ref[...] = o_ref[...] * lax.rsqrt(ms / d + EPS) * fg_ref[...]


def _combine(dest1, dest2, x2, rf, gt, final_g, yb, seq, final):
    t, d = x2.shape
    hw = ROW_SUB * LANES
    tm = 512
    tps = seq // tm
    return pl.pallas_call(
        functools.partial(_combine_kernel, tm=tm, final=final),
        grid_spec=pltpu.PrefetchScalarGridSpec(
            num_scalar_prefetch=2,
            grid=(t // tm,),
            in_specs=[
                pl.BlockSpec((tm, d), lambda i, d1, d2: (i, 0)),
                pl.BlockSpec((tm, LANES), lambda i, d1, d2: (i, 0)),
                pl.BlockSpec((1, 1, d), lambda i, d1, d2: (i // tps, 0, 0)),
                pl.BlockSpec((1, d), lambda i, d1, d2: (0, 0)),
                pl.BlockSpec(memory_space=pl.ANY),
            ],
            out_specs=pl.BlockSpec((tm, d), lambda i, d1, d2: (i, 0)),
            scratch_shapes=[pltpu.VMEM((tm * ROW_SUB, LANES), U32),
                            pltpu.VMEM((tm * ROW_SUB, LANES), U32),
                            pltpu.SemaphoreType.DMA(())],
        ),
        out_shape=jax.ShapeDtypeStruct((t, d), F32),
        compiler_params=_cparams("arbitrary"),
        name="combine",
    )(dest1, dest2, x2, rf, gt, final_g.reshape(1, d), yb)


def _routing_tables(meta, cnt, n_blocks):
    bm = EXPERT_ROWS
    counts = cnt[0, :N_EXPERTS].astype(I32)
    pcounts = (counts + bm - 1) // bm * bm
    pends = jnp.cumsum(pcounts)
    pstarts = pends - pcounts
    eids = jnp.arange(N_EXPERTS, dtype=I32)
    e1, e2, pos1, pos2 = [meta[k].astype(I32) for k in range(4)]
    dest1 = pos1 + jnp.sum(jnp.where(e1[:, None] == eids[None, :], pstarts[None, :], 0), axis=1)
    dest2 = pos2 + jnp.sum(jnp.where(e2[:, None] == eids[None, :], pstarts[None, :], 0), axis=1)
    first_row = jnp.arange(n_blocks, dtype=I32) * bm
    block_exp = jnp.minimum(
        jnp.sum((pends[None, :] <= first_row[:, None]).astype(I32), axis=1), N_EXPERTS - 1)
    n_used = pends[-1:] // bm
    return dest1, dest2, block_exp.astype(I32), n_used.astype(I32)


def kernel(x, c, ada_w, ada_b, norm1_g, w_in, pool_w, pool_scale, conf_dw_w, conf_dw_b,
           conf_ln_g, conf_ln_b, conf_w_out, sc_conv_w, sc_w_out, fnet_w_out, w_o, norm2_g,
           rg_w, rg_b, re_w, re_b, exp_w1, exp_w3, exp_w2, final_g):
    nb, seq, d = x.shape
    depth = ada_w.shape[0]
    t = nb * seq
    mw = d // N_BRANCH
    off_d = 6 * mw
    off_g = 7 * mw
    gc = mw // N_GROUPS
    n_rows = t * 2 + N_EXPERTS * EXPERT_ROWS
    n_blocks = n_rows // EXPERT_ROWS

    mod = _ada_mod(c, ada_w, ada_b)
    band = jnp.asarray(_pool_band(512), BF16)
    tables = _dft_tables(seq, gc)
    x2 = x.reshape(t, d)
    w_in_b = w_in.astype(BF16)
    w_o_b = w_o.astype(BF16)
    assert d // 2 == ROW_SUB * LANES

    for l in range(depth):
        sh1, sc1, gt1, sh2, sc2, gt2 = [mod[l, :, k * d:(k + 1) * d].reshape(nb, 1, d)
                                        for k in range(6)]
        g1 = norm1_g[l].reshape(1, d)
        u = _inproj(x2, g1, sh1, sc1, w_in_b, l, off_g, seq)
        ya, yb, yc = _mixers(u, band, pool_w[l].astype(BF16), pool_scale[l], conf_dw_w[l],
                             conf_dw_b[l], conf_ln_g[l], conf_ln_b[l],
                             conf_w_out[l].astype(BF16), sc_conv_w[l],
                             sc_w_out[l].astype(BF16), seq, d)
        yd = _fourier(u[:, off_d:off_g], tables, fnet_w_out[l].astype(BF16), nb, seq, gc)
        x2 = _merge(x2, g1, sh1, sc1, gt1, (ya, yb, yc, yd), w_in_b, l, off_g, w_o_b, seq)

        wr = jnp.zeros((d, LANES), F32).at[:, :N_EXPERTS].set(re_w[l])
        wr = wr.at[:, N_EXPERTS:N_EXPERTS + N_GROUPS].set(rg_w[l])
        br = jnp.zeros((1, LANES), F32).at[0, :N_EXPERTS].set(re_b[l])
        br = br.at[0, N_EXPERTS:N_EXPERTS + N_GROUPS].set(rg_b[l])
        wr_hi = wr.astype(BF16)
        wr_lo = (wr - wr_hi.astype(F32)).astype(BF16)
        hp, meta, rf, cnt = _router(x2, norm2_g[l].reshape(1, d), sh2, sc2, wr_hi, wr_lo, br, seq)
        dest1, dest2, block_exp, n_used = _routing_tables(meta, cnt, n_blocks)
        xb = _dispatch(dest1, dest2, hp, n_rows)
        ybk = _experts(block_exp, n_used, xb, exp_w1, exp_w3, exp_w2, l)
        x2 = _combine(dest1, dest2, x2, rf, gt2, final_g, ybk, seq, final=(l == depth - 1))
    return x2.reshape(nb, seq, d)
```

```python
import functools

import numpy as np
import jax
import jax.numpy as jnp
from jax import lax
from jax.experimental import pallas as pl
from jax.experimental.pallas import tpu as pltpu

F32 = jnp.float32
BF16 = jnp.bfloat16
U32 = jnp.uint32
I32 = jnp.int32

LANES = 128
EPS = 1e-6
N_BRANCH = 4
POOL_WINDOWS = (2, 4, 8, 16)
CONF_KERNEL = 31
SC_KERNEL = 3
N_GROUPS = 4
EXPERTS_PER_GROUP = 8
N_EXPERTS = N_GROUPS * EXPERTS_PER_GROUP
HALO = 64
EXPERT_ROWS = 256
FFT_N1 = 128
VMEM_LIMIT = 56 * 1024 * 1024
NEG = -1e30


def _cparams(*sem):
    return pltpu.CompilerParams(dimension_semantics=sem, vmem_limit_bytes=VMEM_LIMIT)


def _sigmoid(v):
    return 1.0 / (1.0 + jnp.exp(-v))


def _norm_modulate(x, g, sh, sc):
    ms = jnp.mean(x * x, axis=-1, keepdims=True)
    return (x * lax.rsqrt(ms + EPS)) * (g * (1.0 + sc)) + sh


def _ada_kernel(cb_ref, w_ref, b_ref, o_ref, *, nb, tn):
    w = w_ref[0]
    for b in range(nb):
        cbt = jnp.tile(cb_ref[b], (1, tn // LANES))
        o_ref[0, b:b + 1, :] = jnp.sum(w * cbt, axis=0, keepdims=True) + b_ref[0]


def _ada_mod(c, ada_w, ada_b):
    nl, d, n = ada_w.shape
    nb = c.shape[0]
    tn = 512
    cb = jnp.broadcast_to(c[:, :, None], (nb, d, LANES))
    return pl.pallas_call(
        functools.partial(_ada_kernel, nb=nb, tn=tn),
        grid=(nl, n // tn),
        in_specs=[
            pl.BlockSpec((nb, d, LANES), lambda l, j: (0, 0, 0)),
            pl.BlockSpec((1, d, tn), lambda l, j: (l, 0, j)),
            pl.BlockSpec((1, 1, tn), lambda l, j: (l, 0, j)),
        ],
        out_specs=pl.BlockSpec((1, nb, tn), lambda l, j: (l, 0, j)),
        out_shape=jax.ShapeDtypeStruct((nl, nb, n), F32),
        compiler_params=_cparams("arbitrary", "arbitrary"),
        name="ada_mod",
    )(cb, ada_w, ada_b.reshape(nl, 1, n))


def _inproj_kernel(x_ref, g_ref, sh_ref, sc_ref, w_ref, u_ref, h_ref):
    @pl.when(pl.program_id(1) == 0)
    def _():
        h_ref[...] = _norm_modulate(x_ref[...], g_ref[...], sh_ref[0], sc_ref[0]).astype(BF16)

    u_ref[...] = jnp.dot(h_ref[...], w_ref[0], preferred_element_type=F32).astype(BF16)


def _inproj(x2, g, sh, sc, w_in_b, layer, n, seq):
    t, d = x2.shape
    tm, tn = 1024, 512
    tps = seq // tm
    return pl.pallas_call(
        _inproj_kernel,
        grid=(t // tm, n // tn),
        in_specs=[
            pl.BlockSpec((tm, d), lambda i, j: (i, 0)),
            pl.BlockSpec((1, d), lambda i, j: (0, 0)),
            pl.BlockSpec((1, 1, d), lambda i, j: (i // tps, 0, 0)),
            pl.BlockSpec((1, 1, d), lambda i, j: (i // tps, 0, 0)),
            pl.BlockSpec((1, d, tn), lambda i, j: (layer, 0, j)),
        ],
        out_specs=pl.BlockSpec((tm, tn), lambda i, j: (i, j)),
        out_shape=jax.ShapeDtypeStruct((t, n), BF16),
        scratch_shapes=[pltpu.VMEM((tm, d), BF16)],
        compiler_params=_cparams("arbitrary", "arbitrary"),
        name="inproj",
    )(x2, g, sh, sc, w_in_b)


def _pool_band(tm):
    bands = np.zeros((len(POOL_WINDOWS), tm, tm + 2 * HALO), np.float32)
    for gi, w in enumerate(POOL_WINDOWS):
        left = w // 2
        for i in range(tm):
            lo = i + HALO - left
            bands[gi, i, lo:lo + w] = 1.0
    return bands


def _mixers_kernel(um_ref, up_ref, un_ref, band_ref, poolw_ref, pscale_ref,
                   dww_ref, dwb_ref, lng_ref, lnb_ref, cwo_ref, scw_ref, swo_ref,
                   ya_ref, yb_ref, yc_ref, ub_ref, vb_ref, pb_ref, cv_ref, sh_ref,
                   *, tm, tps, seq, mw):
    ti = pl.program_id(0) % tps
    rows = tm + 2 * HALO
    ub_ref[0:HALO, :] = jnp.where(ti > 0, up_ref[...], jnp.zeros_like(up_ref))
    ub_ref[HALO:HALO + tm, :] = um_ref[...]
    ub_ref[HALO + tm:rows, :] = jnp.where(ti < tps - 1, un_ref[...], jnp.zeros_like(un_ref))

    pos = (ti * tm + lax.broadcasted_iota(I32, (tm, 1), 0)).astype(F32)
    gc = mw // len(POOL_WINDOWS)
    dout = ya_ref.shape[1] // len(POOL_WINDOWS)
    for gi, w in enumerate(POOL_WINDOWS):
        left = w // 2
        right = w - 1 - left
        lo = jnp.clip(pos - left, 0.0, float(seq))
        hi = jnp.clip(pos + (right + 1), 0.0, float(seq))
        cnt = hi - lo
        ug = ub_ref[:, gi * gc:(gi + 1) * gc]
        wsum = jnp.dot(band_ref[gi], ug, preferred_element_type=F32)
        p = wsum / cnt - ug[HALO:HALO + tm, :].astype(F32)
        y = jnp.dot(p.astype(BF16), poolw_ref[gi], preferred_element_type=F32)
        ya_ref[:, gi * dout:(gi + 1) * dout] = (
            y * pscale_ref[:, gi * dout:(gi + 1) * dout]).astype(BF16)

    a = ub_ref[:, mw:2 * mw].astype(F32)
    gate = ub_ref[:, 2 * mw:3 * mw].astype(F32)
    vb_ref[...] = a * _sigmoid(gate)
    rc = 128
    base = HALO - CONF_KERNEL // 2 - 1
    assert base % 8 == 0 and sh_ref.shape[1] >= tm + 8 * (CONF_KERNEL // 8)
    nsh = sh_ref.shape[1]
    for c0 in range(0, mw, LANES):
        for r in range(1, 8):
            sh_ref[r] = vb_ref[base + r:base + r + nsh, c0:c0 + LANES]
        for r0 in range(0, tm, rc):
            acc = jnp.zeros((rc, LANES), F32)
            for k in range(CONF_KERNEL):
                q, r = divmod(k + 1, 8)
                row = r0 + 8 * q
                if r == 0:
                    src = vb_ref[base + row:base + row + rc, c0:c0 + LANES]
                else:
                    src = sh_ref[r, row:row + rc, :]
                acc = acc + dww_ref[k:k + 1, c0:c0 + LANES] * src
            cv_ref[r0:r0 + rc, c0:c0 + LANES] = acc
    v = cv_ref[...] + dwb_ref[...]
    mu = jnp.mean(v, axis=-1, keepdims=True)
    dv = v - mu
    var = jnp.mean(dv * dv, axis=-1, keepdims=True)
    ln = dv * lax.rsqrt(var + EPS) * lng_ref[...] + lnb_ref[...]
    act = ln * _sigmoid(ln)
    yb_ref[...] = jnp.dot(act.astype(BF16), cwo_ref[...], preferred_element_type=F32).astype(BF16)

    pb_ref[...] = ub_ref[:, 4 * mw:5 * mw].astype(F32) * ub_ref[:, 5 * mw:6 * mw].astype(F32)
    z = jnp.zeros((tm, mw), F32)
    for k in range(SC_KERNEL):
        off = HALO - SC_KERNEL // 2 + k
        z = z + scw_ref[k:k + 1, :] * pb_ref[off:off + tm, :]
    bg = ub_ref[HALO:HALO + tm, 3 * mw:4 * mw].astype(F32)
    yc_ref[...] = jnp.dot((bg * z).astype(BF16), swo_ref[...], preferred_element_type=F32).astype(BF16)


def _mixers(u, band, pool_w, pool_scale, dw_w, dw_b, ln_g, ln_b, conf_wo, sc_w, sc_wo, seq, d):
    t = u.shape[0]
    mw = pool_w.shape[0] * pool_w.shape[1]
    tm = band.shape[1]
    tps = seq // tm
    hb = tm // HALO
    nhalo = t // HALO
    wabc = 6 * mw
    full = lambda shape: pl.BlockSpec(shape, lambda i: (0,) * len(shape))
    out_sds = jax.ShapeDtypeStruct((t, d), BF16)
    return pl.pallas_call(
        functools.partial(_mixers_kernel, tm=tm, tps=tps, seq=seq, mw=mw),
        grid=(t // tm,),
        in_specs=[
            pl.BlockSpec((tm, wabc), lambda i: (i, 0)),
            pl.BlockSpec((HALO, wabc), lambda i: (jnp.maximum(i * hb - 1, 0), 0)),
            pl.BlockSpec((HALO, wabc), lambda i: (jnp.minimum((i + 1) * hb, nhalo - 1), 0)),
            full(band.shape), full(pool_w.shape), full((1, d)),
            full(dw_w.shape), full((1, mw)), full((1, mw)), full((1, mw)),
            full(conf_wo.shape), full(sc_w.shape), full(sc_wo.shape),
        ],
        out_specs=[pl.BlockSpec((tm, d), lambda i: (i, 0))] * 3,
        out_shape=[out_sds] * 3,
        scratch_shapes=[
            pltpu.VMEM((tm + 2 * HALO, wabc), BF16),
            pltpu.VMEM((tm + 2 * HALO, mw), F32),
            pltpu.VMEM((tm + 2 * HALO, mw), F32),
            pltpu.VMEM((tm, mw), F32),
            pltpu.VMEM((8, tm + 32, LANES), F32),
        ],
        compiler_params=_cparams("arbitrary"),
        name="mixers",
    )(u, u, u, band, pool_w, pool_scale.reshape(1, d), dw_w, dw_b.reshape(1, mw),
      ln_g.reshape(1, mw), ln_b.reshape(1, mw), conf_wo, sc_w, sc_wo)


def _dft_tables(seq, gc):
    n1, n2 = FFT_N1, seq // FFT_N1
    k2 = np.arange(n2)[:, None] * np.arange(n2)[None, :] * (2 * np.pi / n2)
    f2 = np.concatenate([np.cos(k2), -np.sin(k2)], axis=0)
    k1 = np.arange(n1)[:, None] * np.arange(n1)[None, :] * (2 * np.pi / n1)
    c1, s1 = np.cos(k1), np.sin(k1)
    m1 = np.block([[c1, s1], [-s1, c1]])
    tw = np.arange(n2)[:, None] * np.arange(n1)[None, :] * (2 * np.pi / seq)
    twc = np.broadcast_to(np.cos(tw)[:, :, None], (n2, n1, LANES))
    tws = np.broadcast_to(np.sin(tw)[:, :, None], (n2, n1, LANES))
    kc = np.arange(gc)[:, None] * np.arange(gc)[None, :] * (2 * np.pi / gc)
    mc = np.concatenate([np.cos(kc), np.sin(kc)], axis=0) / np.sqrt(seq * gc)
    return (jnp.asarray(f2, BF16), jnp.asarray(m1, BF16), jnp.asarray(twc, F32),
            jnp.asarray(tws, F32), jnp.asarray(mc, BF16))


def _fft1_kernel(f2_ref, x_ref, a_ref):
    a_ref[0] = jnp.dot(f2_ref[...], x_ref[0], preferred_element_type=F32).astype(BF16)


def _fft2_kernel(a_ref, twc_ref, tws_ref, m1_ref, mc_ref, wo_ref, y_ref, f_ref, *, kb, gc, d):
    n1 = m1_ref.shape[0] // 2
    mw = a_ref.shape[-1]
    cw = 512
    for j in range(kb):
        ar = a_ref[0, 0, j].astype(F32)
        ai = a_ref[0, 1, j].astype(F32)
        tc = jnp.tile(twc_ref[j], (1, mw // LANES))
        ts = jnp.tile(tws_ref[j], (1, mw // LANES))
        br = ar * tc + ai * ts
        bi = ai * tc - ar * ts
        bb = jnp.concatenate([br, bi], axis=0).astype(BF16)
        xx = jnp.dot(m1_ref[...], bb, preferred_element_type=F32)
        xr = xx[0:n1].astype(BF16)
        xi = xx[n1:2 * n1].astype(BF16)
        for g in range(mw // gc):
            lhs = jnp.concatenate([xr[:, g * gc:(g + 1) * gc], xi[:, g * gc:(g + 1) * gc]], axis=1)
            f_ref[g, pl.ds(j, n1, stride=kb), :] = jnp.dot(
                lhs, mc_ref[...], preferred_element_type=F32)
    f = jnp.concatenate([f_ref[g] for g in range(mw // gc)], axis=1).astype(BF16)
    for c0 in range(0, d, cw):
        yc = jnp.dot(f, wo_ref[:, c0:c0 + cw], preferred_element_type=F32)
        y_ref[0, :, :, c0:c0 + cw] = yc.reshape(n1, kb, cw).astype(BF16)


def _fourier(ud, tables, w_out, nb, seq, gc):
    f2, m1, twc, tws, mc = tables
    mw = ud.shape[1]
    d = w_out.shape[1]
    n1, n2 = FFT_N1, seq // FFT_N1
    xv = ud.reshape(nb, n2, n1 * mw)
    tn = 8192
    a = pl.pallas_call(
        _fft1_kernel,
        grid=(nb, n1 * mw // tn),
        in_specs=[
            pl.BlockSpec((2 * n2, n2), lambda b, j: (0, 0)),
            pl.BlockSpec((1, n2, tn), lambda b, j: (b, 0, j)),
        ],
        out_specs=pl.BlockSpec((1, 2 * n2, tn), lambda b, j: (b, 0, j)),
        out_shape=jax.ShapeDtypeStruct((nb, 2 * n2, n1 * mw), BF16),
        compiler_params=_cparams("arbitrary", "arbitrary"),
        name="fft_stage1",
    )(f2, xv)
    a5 = a.reshape(nb, 2, n2, n1, mw)
    kb = 16
    y = pl.pallas_call(
        functools.partial(_fft2_kernel, kb=kb, gc=gc, d=d),
        grid=(nb, n2 // kb),
        in_specs=[
            pl.BlockSpec((1, 2, kb, n1, mw), lambda b, k: (b, 0, k, 0, 0)),
            pl.BlockSpec((kb, n1, LANES), lambda b, k: (k, 0, 0)),
            pl.BlockSpec((kb, n1, LANES), lambda b, k: (k, 0, 0)),
            pl.BlockSpec(m1.shape, lambda b, k: (0, 0)),
            pl.BlockSpec(mc.shape, lambda b, k: (0, 0)),
            pl.BlockSpec(w_out.shape, lambda b, k: (0, 0)),
        ],
        out_specs=pl.BlockSpec((1, n1, kb, d), lambda b, k: (b, 0, k, 0)),
        out_shape=jax.ShapeDtypeStruct((nb, n1, n2, d), BF16),
        scratch_shapes=[pltpu.VMEM((mw // gc, n1 * kb, gc), F32)],
        compiler_params=_cparams("arbitrary", "arbitrary"),
        name="fft_stage2",
    )(a5, twc, tws, m1, mc, w_out)
    return y.reshape(nb * seq, d)


def _merge_kernel(x_ref, g_ref, sh_ref, sc_ref, gt_ref, ya_ref, yb_ref, yc_ref, yd_ref,
                  wa_ref, wb_ref, wc_ref, wd_ref, wo_ref, o_ref, h_ref, acc_ref):
    j = pl.program_id(1)

    @pl.when(j == 0)
    def _():
        h_ref[...] = _norm_modulate(x_ref[...], g_ref[...], sh_ref[0], sc_ref[0]).astype(BF16)
        acc_ref[...] = jnp.zeros_like(acc_ref)

    h = h_ref[...]
    merged = None
    for w_ref, y_ref in ((wa_ref, ya_ref), (wb_ref, yb_ref), (wc_ref, yc_ref), (wd_ref, yd_ref)):
        gl = jnp.dot(h, w_ref[0], preferred_element_type=F32)
        term = _sigmoid(gl) * y_ref[...].astype(F32)
        merged = term if merged is None else merged + term
    acc_ref[...] += jnp.dot(merged.astype(BF16), wo_ref[0], preferred_element_type=F32)

    @pl.when(j == pl.num_programs(1) - 1)
    def _():
        o_ref[...] = x_ref[...] + gt_ref[0] * acc_ref[...]


def _merge(x2, g, sh, sc, gt, ys, w_in_b, layer, off_g, w_o_b, seq):
    t, d = x2.shape
    tm, tc = 512, 512
    tps = seq // tm
    nj = d // tc
    gb = off_g // tc
    mod_spec = pl.BlockSpec((1, 1, d), lambda i, j: (i // tps, 0, 0))
    y_spec = pl.BlockSpec((tm, tc), lambda i, j: (i, j))
    w_specs = [pl.BlockSpec((1, d, tc),
                            functools.partial(lambda i, j, k: (layer, 0, gb + k * nj + j), k=k))
               for k in range(N_BRANCH)]
    return pl.pallas_call(
        _merge_kernel,
        grid=(t // tm, nj),
        in_specs=[
            pl.BlockSpec((tm, d), lambda i, j: (i, 0)),
            pl.BlockSpec((1, d), lambda i, j: (0, 0)),
            mod_spec, mod_spec, mod_spec,
            y_spec, y_spec, y_spec, y_spec,
            *w_specs,
            pl.BlockSpec((1, tc, d), lambda i, j: (layer, j, 0)),
        ],
        out_specs=pl.BlockSpec((tm, d), lambda i, j: (i, 0)),
        out_shape=jax.ShapeDtypeStruct((t, d), F32),
        scratch_shapes=[pltpu.VMEM((tm, d), BF16), pltpu.VMEM((tm, d), F32)],
        compiler_params=_cparams("arbitrary", "arbitrary"),
        name="merge",
    )(x2, g, sh, sc, gt, *ys, w_in_b, w_in_b, w_in_b, w_in_b, w_o_b)


ROW_SUB = 8


def _store_rows(ref, v):
    m, d = v.shape
    half = d // 2
    for s in range(ROW_SUB):
        hi = lax.bitcast_convert_type(
            v[:, s * LANES:(s + 1) * LANES].astype(BF16).astype(F32), U32)
        lo = lax.bitcast_convert_type(
            v[:, half + s * LANES:half + (s + 1) * LANES].astype(BF16).astype(F32), U32)
        ref[pl.ds(s, m, stride=ROW_SUB), :] = hi | (lo >> 16)


def _load_rows(ref, m, s):
    p = ref[pl.ds(s, m, stride=ROW_SUB), :]
    hi = lax.bitcast_convert_type(p & jnp.uint32(0xFFFF0000), F32)
    lo = lax.bitcast_convert_type(p << 16, F32)
    return hi, lo


def _router_kernel(x_ref, g_ref, sh_ref, sc_ref, whi_ref, wlo_ref, b_ref,
                   hp_ref, meta_ref, rf_ref, cnt_ref, base_ref, *, tm):
    i = pl.program_id(0)

    @pl.when(i == 0)
    def _():
        base_ref[...] = jnp.zeros_like(base_ref)

    h = _norm_modulate(x_ref[...], g_ref[...], sh_ref[0], sc_ref[0])
    _store_rows(hp_ref, h)
    h_hi = h.astype(BF16)
    h_lo = (h - h_hi.astype(F32)).astype(BF16)
    whi = whi_ref[...]
    logits = (jnp.dot(h_hi, whi, preferred_element_type=F32)
              + jnp.dot(h_lo, whi, preferred_element_type=F32)
              + jnp.dot(h_hi, wlo_ref[...], preferred_element_type=F32)) + b_ref[...]

    lane = lax.broadcasted_iota(I32, (tm, LANES), 1)
    lane_f = lane.astype(F32)
    is_grp = (lane >= N_EXPERTS) & (lane < N_EXPERTS + N_GROUPS)
    lg = jnp.where(is_grp, logits, NEG)
    mg = jnp.max(lg, axis=1, keepdims=True)
    gidx = jnp.min(jnp.where(lg == mg, lane_f - N_EXPERTS, 1e9), axis=1, keepdims=True)
    pg_top = 1.0 / jnp.sum(jnp.where(is_grp, jnp.exp(lg - mg), 0.0), axis=1, keepdims=True)

    in_grp = (lane < N_EXPERTS) & (jnp.right_shift(lane, 3).astype(F32) == gidx)
    le = jnp.where(in_grp, logits, NEG)
    m1 = jnp.max(le, axis=1, keepdims=True)
    e1 = jnp.min(jnp.where(le == m1, lane_f, 1e9), axis=1, keepdims=True)
    le2 = jnp.where(lane_f == e1, NEG, le)
    m2 = jnp.max(le2, axis=1, keepdims=True)
    e2 = jnp.min(jnp.where(le2 == m2, lane_f, 1e9), axis=1, keepdims=True)
    w1 = 1.0 / (1.0 + jnp.exp(m2 - m1))
    w2 = 1.0 - w1

    is1 = lane_f == e1
    is2 = lane_f == e2
    oh = jnp.where(is1 | is2, 1.0, 0.0)
    r_io = lax.broadcasted_iota(I32, (tm, tm), 0)
    c_io = lax.broadcasted_iota(I32, (tm, tm), 1)
    tri = jnp.where(r_io > c_io, 1.0, 0.0).astype(BF16)
    prior = jnp.dot(tri, oh.astype(BF16), preferred_element_type=F32) + base_ref[0:1, :]
    pos1 = jnp.sum(jnp.where(is1, prior, 0.0), axis=1, keepdims=True)
    pos2 = jnp.sum(jnp.where(is2, prior, 0.0), axis=1, keepdims=True)
    base_ref[0:1, :] = base_ref[0:1, :] + jnp.sum(oh, axis=0, keepdims=True)

    ints = jnp.where(lane == 0, e1, jnp.where(lane == 1, e2, jnp.where(lane == 2, pos1, pos2)))
    meta_ref[...] = jnp.transpose(ints)[0:8, :]
    rf_ref[...] = jnp.where(lane == 0, w1 * pg_top, w2 * pg_top)
    cnt_ref[...] = base_ref[...]


def _router(x2, g, sh, sc, w_hi, w_lo, bias, seq):
    t, d = x2.shape
    tm = 512
    tps = seq // tm
    mod_spec = pl.BlockSpec((1, 1, d), lambda i: (i // tps, 0, 0))
    row_spec = pl.BlockSpec((tm, LANES), lambda i: (i, 0))
    return pl.pallas_call(
        functools.partial(_router_kernel, tm=tm),
        grid=(t // tm,),
        in_specs=[
            pl.BlockSpec((tm, d), lambda i: (i, 0)),
            pl.BlockSpec((1, d), lambda i: (0, 0)),
            mod_spec, mod_spec,
            pl.BlockSpec((d, LANES), lambda i: (0, 0)),
            pl.BlockSpec((d, LANES), lambda i: (0, 0)),
            pl.BlockSpec((1, LANES), lambda i: (0, 0)),
        ],
        out_specs=[pl.BlockSpec((tm * ROW_SUB, LANES), lambda i: (i, 0)),
                   pl.BlockSpec((8, tm), lambda i: (0, i)),
                   row_spec,
                   pl.BlockSpec((8, LANES), lambda i: (0, 0))],
        out_shape=[jax.ShapeDtypeStruct((t * ROW_SUB, LANES), U32),
                   jax.ShapeDtypeStruct((8, t), F32),
                   jax.ShapeDtypeStruct((t, LANES), F32),
                   jax.ShapeDtypeStruct((8, LANES), F32)],
        scratch_shapes=[pltpu.VMEM((8, LANES), F32)],
        compiler_params=_cparams("arbitrary"),
        name="router",
    )(x2, g, sh, sc, w_hi, w_lo, bias)


def _dispatch_kernel(d1_ref, d2_ref, h_ref, xb_in_ref, xb_ref, sem, *, tm):
    del xb_in_ref
    base = pl.program_id(0) * tm

    def row_copy(r, dst):
        return pltpu.make_async_copy(
            h_ref.at[pl.ds(pl.multiple_of(r * ROW_SUB, ROW_SUB), ROW_SUB), :],
            xb_ref.at[pl.ds(pl.multiple_of(dst * ROW_SUB, ROW_SUB), ROW_SUB), :], sem)

    def issue(r, carry):
        row_copy(r, d1_ref[base + r]).start()
        row_copy(r, d2_ref[base + r]).start()
        return carry

    lax.fori_loop(0, tm, issue, 0, unroll=8)

    def drain(r, carry):
        row_copy(r, 0).wait()
        row_copy(r, 0).wait()
        return carry

    lax.fori_loop(0, tm, drain, 0, unroll=8)


def _dispatch(dest1, dest2, hp, n_rows):
    t = hp.shape[0] // ROW_SUB
    tm = 512
    xb0 = jnp.zeros((n_rows * ROW_SUB, LANES), U32)
    return pl.pallas_call(
        functools.partial(_dispatch_kernel, tm=tm),
        grid_spec=pltpu.PrefetchScalarGridSpec(
            num_scalar_prefetch=2,
            grid=(t // tm,),
            in_specs=[pl.BlockSpec((tm * ROW_SUB, LANES), lambda i, d1, d2: (i, 0)),
                      pl.BlockSpec(memory_space=pl.ANY)],
            out_specs=pl.BlockSpec(memory_space=pl.ANY),
            scratch_shapes=[pltpu.SemaphoreType.DMA(())],
        ),
        out_shape=jax.ShapeDtypeStruct((n_rows * ROW_SUB, LANES), U32),
        input_output_aliases={3: 0},
        compiler_params=_cparams("arbitrary"),
        name="dispatch",
    )(dest1, dest2, hp, xb0)


def _experts_kernel(be_ref, nxt_ref, slot_ref, nu_ref, xb_ref, w1_hbm, w3_hbm, w2_hbm, yb_ref,
                    w1f, w3f, w2f, w1b, w3b, w2b, sem, *, layer):
    b = pl.program_id(0)
    e = be_ref[b]
    slot = slot_ref[b]
    first = (b == 0) | (e != be_ref[jnp.maximum(b - 1, 0)])

    def fetch(exp, s):
        return (pltpu.make_async_copy(w1_hbm.at[layer, exp], w1f.at[s], sem.at[s, 0]),
                pltpu.make_async_copy(w3_hbm.at[layer, exp], w3f.at[s], sem.at[s, 1]),
                pltpu.make_async_copy(w2_hbm.at[layer, exp], w2f.at[s], sem.at[s, 2]))

    @pl.when(b == 0)
    def _():
        for cp in fetch(e, slot):
            cp.start()

    @pl.when(first)
    def _():
        for cp in fetch(e, slot):
            cp.wait()
        w1b[...] = w1f[slot].astype(BF16)
        w3b[...] = w3f[slot].astype(BF16)
        w2b[...] = w2f[slot].astype(BF16)

        @pl.when(nxt_ref[b] >= 0)
        def _():
            for cp in fetch(nxt_ref[b], 1 - slot):
                cp.start()

    @pl.when(b < nu_ref[0])
    def _():
        bm = xb_ref.shape[0] // ROW_SUB
        parts = [_load_rows(xb_ref, bm, s) for s in range(ROW_SUB)]
        xrow = jnp.concatenate([p[0].astype(BF16) for p in parts]
                               + [p[1].astype(BF16) for p in parts], axis=1)
        h1 = jnp.dot(xrow, w1b[...], preferred_element_type=F32)
        h3 = jnp.dot(xrow, w3b[...], preferred_element_type=F32)
        hh = (h1 * _sigmoid(h1) * h3).astype(BF16)
        _store_rows(yb_ref, jnp.dot(hh, w2b[...], preferred_element_type=F32))

    @pl.when(b >= nu_ref[0])
    def _():
        yb_ref[...] = jnp.zeros_like(yb_ref)


def _experts(block_exp, next_exp, slot, n_used, xb, w1, w3, w2, layer):
    n_rows = xb.shape[0] // ROW_SUB
    _, _, d, de = w1.shape
    bm = EXPERT_ROWS
    row_spec = pl.BlockSpec((bm * ROW_SUB, LANES), lambda b, *_: (b, 0))
    hbm_spec = pl.BlockSpec(memory_space=pl.ANY)
    return pl.pallas_call(
        functools.partial(_experts_kernel, layer=layer),
        grid_spec=pltpu.PrefetchScalarGridSpec(
            num_scalar_prefetch=4,
            grid=(n_rows // bm,),
            in_specs=[row_spec, hbm_spec, hbm_spec, hbm_spec],
            out_specs=row_spec,
            scratch_shapes=[pltpu.VMEM((2, d, de), F32), pltpu.VMEM((2, d, de), F32),
                            pltpu.VMEM((2, de, d), F32),
                            pltpu.VMEM((d, de), BF16), pltpu.VMEM((d, de), BF16),
                            pltpu.VMEM((de, d), BF16),
                            pltpu.SemaphoreType.DMA((2, 3))],
        ),
        out_shape=jax.ShapeDtypeStruct((n_rows * ROW_SUB, LANES), U32),
        compiler_params=_cparams("arbitrary"),
        name="experts",
    )(block_exp, next_exp, slot, n_used, xb, w1, w3, w2)


def _combine_kernel(d1_ref, d2_ref, x_ref, rf_ref, gt_ref, fg_ref, yb_ref, o_ref,
                    r1_ref, r2_ref, sem, *, tm, final):
    base = pl.program_id(0) * tm

    def row_copy(src, buf, r):
        return pltpu.make_async_copy(
            yb_ref.at[pl.ds(pl.multiple_of(src * ROW_SUB, ROW_SUB), ROW_SUB), :],
            buf.at[pl.ds(pl.multiple_of(r * ROW_SUB, ROW_SUB), ROW_SUB), :], sem)

    def issue(r, carry):
        row_copy(d1_ref[base + r], r1_ref, r).start()
        row_copy(d2_ref[base + r], r2_ref, r).start()
        return carry

    lax.fori_loop(0, tm, issue, 0, unroll=8)

    def drain(r, carry):
        row_copy(0, r1_ref, r).wait()
        row_copy(0, r2_ref, r).wait()
        return carry

    lax.fori_loop(0, tm, drain, 0, unroll=8)

    d = x_ref.shape[1]
    half = d // 2
    w1 = rf_ref[:, 0:1]
    w2 = rf_ref[:, 1:2]
    ms = jnp.zeros((tm, 1), F32)
    for s in range(ROW_SUB):
        a_hi, a_lo = _load_rows(r1_ref, tm, s)
        b_hi, b_lo = _load_rows(r2_ref, tm, s)
        for c0, ya, yb2 in ((s * LANES, a_hi, b_hi), (half + s * LANES, a_lo, b_lo)):
            xn = x_ref[:, c0:c0 + LANES] + gt_ref[0, :, c0:c0 + LANES] * (ya * w1 + yb2 * w2)
            o_ref[:, c0:c0 + LANES] = xn
            if final:
                ms = ms + jnp.sum(xn * xn, axis=-1, keepdims=True)
    if final:
        o_ref[...] = o_ref[...] * lax.rsqrt(ms / d + EPS) * fg_ref[...]


def _combine(dest1, dest2, x2, rf, gt, final_g, yb, seq, final):
    t, d = x2.shape
    hw = ROW_SUB * LANES
    tm = 512
    tps = seq // tm
    return pl.pallas_call(
        functools.partial(_combine_kernel, tm=tm, final=final),
        grid_spec=pltpu.PrefetchScalarGridSpec(
            num_scalar_prefetch=2,
            grid=(t // tm,),
            in_specs=[
                pl.BlockSpec((tm, d), lambda i, d1, d2: (i, 0)),
                pl.BlockSpec((tm, LANES), lambda i, d1, d2: (i, 0)),
                pl.BlockSpec((1, 1, d), lambda i, d1, d2: (i // tps, 0, 0)),
                pl.BlockSpec((1, d), lambda i, d1, d2: (0, 0)),
                pl.BlockSpec(memory_space=pl.ANY),
            ],
            out_specs=pl.BlockSpec((tm, d), lambda i, d1, d2: (i, 0)),
            scratch_shapes=[pltpu.VMEM((tm * ROW_SUB, LANES), U32),
                            pltpu.VMEM((tm * ROW_SUB, LANES), U32),
                            pltpu.SemaphoreType.DMA(())],
        ),
        out_shape=jax.ShapeDtypeStruct((t, d), F32),
        compiler_params=_cparams("arbitrary"),
        name="combine",
    )(dest1, dest2, x2, rf, gt, final_g.reshape(1, d), yb)


def _routing_tables(meta, cnt, n_blocks):
    bm = EXPERT_ROWS
    counts = cnt[0, :N_EXPERTS].astype(I32)
    pcounts = (counts + bm - 1) // bm * bm
    pends = jnp.cumsum(pcounts)
    pstarts = pends - pcounts
    eids = jnp.arange(N_EXPERTS, dtype=I32)
    e1, e2, pos1, pos2 = [meta[k].astype(I32) for k in range(4)]
    dest1 = pos1 + jnp.sum(jnp.where(e1[:, None] == eids[None, :], pstarts[None, :], 0), axis=1)
    dest2 = pos2 + jnp.sum(jnp.where(e2[:, None] == eids[None, :], pstarts[None, :], 0), axis=1)
    first_row = jnp.arange(n_blocks, dtype=I32) * bm
    block_exp = jnp.minimum(
        jnp.sum((pends[None, :] <= first_row[:, None]).astype(I32), axis=1), N_EXPERTS - 1)
    n_used = pends[-1:] // bm
    blk = jnp.arange(n_blocks, dtype=I32)
    block_exp = jnp.where(blk < n_used[0], block_exp, jnp.max(jnp.where(counts > 0, eids, 0)))
    first = jnp.concatenate([jnp.ones((1,), bool), block_exp[1:] != block_exp[:-1]])
    slot = (jnp.cumsum(first.astype(I32)) - 1) % 2
    later = jnp.where(block_exp[None, :] > block_exp[:, None], block_exp[None, :], N_EXPERTS)
    next_exp = jnp.min(later, axis=1)
    next_exp = jnp.where(next_exp == N_EXPERTS, -1, next_exp)
    return (dest1, dest2, block_exp.astype(I32), next_exp.astype(I32), slot.astype(I32),
            n_used.astype(I32))


def kernel(x, c, ada_w, ada_b, norm1_g, w_in, pool_w, pool_scale, conf_dw_w, conf_dw_b,
           conf_ln_g, conf_ln_b, conf_w_out, sc_conv_w, sc_w_out, fnet_w_out, w_o, norm2_g,
           rg_w, rg_b, re_w, re_b, exp_w1, exp_w3, exp_w2, final_g):
    nb, seq, d = x.shape
    depth = ada_w.shape[0]
    t = nb * seq
    mw = d // N_BRANCH
    off_d = 6 * mw
    off_g = 7 * mw
    gc = mw // N_GROUPS
    n_rows = t * 2 + N_EXPERTS * EXPERT_ROWS
    n_blocks = n_rows // EXPERT_ROWS

    mod = _ada_mod(c, ada_w, ada_b)
    band = jnp.asarray(_pool_band(512), BF16)
    tables = _dft_tables(seq, gc)
    x2 = x.reshape(t, d)
    w_in_b = w_in.astype(BF16)
    w_o_b = w_o.astype(BF16)
    assert d // 2 == ROW_SUB * LANES

    for l in range(depth):
        sh1, sc1, gt1, sh2, sc2, gt2 = [mod[l, :, k * d:(k + 1) * d].reshape(nb, 1, d)
                                        for k in range(6)]
        g1 = norm1_g[l].reshape(1, d)
        u = _inproj(x2, g1, sh1, sc1, w_in_b, l, off_g, seq)
        ya, yb, yc = _mixers(u, band, pool_w[l].astype(BF16), pool_scale[l], conf_dw_w[l],
                             conf_dw_b[l], conf_ln_g[l], conf_ln_b[l],
                             conf_w_out[l].astype(BF16), sc_conv_w[l],
                             sc_w_out[l].astype(BF16), seq, d)
        yd = _fourier(u[:, off_d:off_g], tables, fnet_w_out[l].astype(BF16), nb, seq, gc)
        x2 = _merge(x2, g1, sh1, sc1, gt1, (ya, yb, yc, yd), w_in_b, l, off_g, w_o_b, seq)

        wr = jnp.zeros((d, LANES), F32).at[:, :N_EXPERTS].set(re_w[l])
        wr = wr.at[:, N_EXPERTS:N_EXPERTS + N_GROUPS].set(rg_w[l])
        br = jnp.zeros((1, LANES), F32).at[0, :N_EXPERTS].set(re_b[l])
        br = br.at[0, N_EXPERTS:N_EXPERTS + N_GROUPS].set(rg_b[l])
        wr_hi = wr.astype(BF16)
        wr_lo = (wr - wr_hi.astype(F32)).astype(BF16)
        hp, meta, rf, cnt = _router(x2, norm2_g[l].reshape(1, d), sh2, sc2, wr_hi, wr_lo, br, seq)
        dest1, dest2, block_exp, next_exp, slot, n_used = _routing_tables(meta, cnt, n_blocks)
        xb = _dispatch(dest1, dest2, hp, n_rows)
        ybk = _experts(block_exp, next_exp, slot, n_used, xb, exp_w1, exp_w3, exp_w2, l)
        x2 = _combine(dest1, dest2, x2, rf, gt2, final_g, ybk, seq, final=(l == depth - 1))
    return x2.reshape(nb, seq, d)
```

```python
import functools

import numpy as np
import jax
import jax.numpy as jnp
from jax import lax
from jax.experimental import pallas as pl
from jax.experimental.pallas import tpu as pltpu

F32 = jnp.float32
BF16 = jnp.bfloat16
U32 = jnp.uint32
I32 = jnp.int32

LANES = 128
EPS = 1e-6
N_BRANCH = 4
POOL_WINDOWS = (2, 4, 8, 16)
CONF_KERNEL = 31
SC_KERNEL = 3
N_GROUPS = 4
EXPERTS_PER_GROUP = 8
N_EXPERTS = N_GROUPS * EXPERTS_PER_GROUP
HALO = 64
EXPERT_ROWS = 256
FFT_N1 = 128
VMEM_LIMIT = 56 * 1024 * 1024
NEG = -1e30


def _cparams(*sem):
    return pltpu.CompilerParams(dimension_semantics=sem, vmem_limit_bytes=VMEM_LIMIT)


def _sigmoid(v):
    return 1.0 / (1.0 + jnp.exp(-v))


def _norm_modulate(x, g, sh, sc):
    ms = jnp.mean(x * x, axis=-1, keepdims=True)
    return (x * lax.rsqrt(ms + EPS)) * (g * (1.0 + sc)) + sh


def _ada_kernel(cb_ref, w_ref, b_ref, o_ref, *, nb, tn):
    w = w_ref[0]
    for b in range(nb):
        cbt = jnp.tile(cb_ref[b], (1, tn // LANES))
        o_ref[0, b:b + 1, :] = jnp.sum(w * cbt, axis=0, keepdims=True) + b_ref[0]


def _ada_mod(c, ada_w, ada_b):
    nl, d, n = ada_w.shape
    nb = c.shape[0]
    tn = 512
    cb = jnp.broadcast_to(c[:, :, None], (nb, d, LANES))
    return pl.pallas_call(
        functools.partial(_ada_kernel, nb=nb, tn=tn),
        grid=(nl, n // tn),
        in_specs=[
            pl.BlockSpec((nb, d, LANES), lambda l, j: (0, 0, 0)),
            pl.BlockSpec((1, d, tn), lambda l, j: (l, 0, j)),
            pl.BlockSpec((1, 1, tn), lambda l, j: (l, 0, j)),
        ],
        out_specs=pl.BlockSpec((1, nb, tn), lambda l, j: (l, 0, j)),
        out_shape=jax.ShapeDtypeStruct((nl, nb, n), F32),
        compiler_params=_cparams("arbitrary", "arbitrary"),
        name="ada_mod",
    )(cb, ada_w, ada_b.reshape(nl, 1, n))


def _inproj_kernel(x_ref, g_ref, sh_ref, sc_ref, w_ref, u_ref, h_ref):
    @pl.when(pl.program_id(1) == 0)
    def _():
        h_ref[...] = _norm_modulate(x_ref[...], g_ref[...], sh_ref[0], sc_ref[0]).astype(BF16)

    u_ref[...] = jnp.dot(h_ref[...], w_ref[0], preferred_element_type=F32).astype(BF16)


def _inproj(x2, g, sh, sc, w_in_b, layer, n, seq):
    t, d = x2.shape
    tm, tn = 1024, 512
    tps = seq // tm
    return pl.pallas_call(
        _inproj_kernel,
        grid=(t // tm, n // tn),
        in_specs=[
            pl.BlockSpec((tm, d), lambda i, j: (i, 0)),
            pl.BlockSpec((1, d), lambda i, j: (0, 0)),
            pl.BlockSpec((1, 1, d), lambda i, j: (i // tps, 0, 0)),
            pl.BlockSpec((1, 1, d), lambda i, j: (i // tps, 0, 0)),
            pl.BlockSpec((1, d, tn), lambda i, j: (layer, 0, j)),
        ],
        out_specs=[pl.BlockSpec((tm, tn), lambda i, j: (i, j)),
                   pl.BlockSpec((tm, d), lambda i, j: (i, 0))],
        out_shape=[jax.ShapeDtypeStruct((t, n), BF16), jax.ShapeDtypeStruct((t, d), BF16)],
        compiler_params=_cparams("arbitrary", "arbitrary"),
        name="inproj",
    )(x2, g, sh, sc, w_in_b)


def _pool_band(tm):
    bands = np.zeros((len(POOL_WINDOWS), tm, tm + 2 * HALO), np.float32)
    for gi, w in enumerate(POOL_WINDOWS):
        left = w // 2
        for i in range(tm):
            lo = i + HALO - left
            bands[gi, i, lo:lo + w] = 1.0
    return bands


def _mixers_kernel(um_ref, up_ref, un_ref, band_ref, poolw_ref, pscale_ref,
                   dww_ref, dwb_ref, lng_ref, lnb_ref, cwo_ref, scw_ref, swo_ref,
                   ya_ref, yb_ref, yc_ref, ub_ref, vb_ref, pb_ref, cv_ref, sh_ref,
                   *, tm, tps, seq, mw):
    ti = pl.program_id(0) % tps
    rows = tm + 2 * HALO
    ub_ref[0:HALO, :] = jnp.where(ti > 0, up_ref[...], jnp.zeros_like(up_ref))
    ub_ref[HALO:HALO + tm, :] = um_ref[...]
    ub_ref[HALO + tm:rows, :] = jnp.where(ti < tps - 1, un_ref[...], jnp.zeros_like(un_ref))

    pos = (ti * tm + lax.broadcasted_iota(I32, (tm, 1), 0)).astype(F32)
    gc = mw // len(POOL_WINDOWS)
    dout = ya_ref.shape[1] // len(POOL_WINDOWS)
    for gi, w in enumerate(POOL_WINDOWS):
        left = w // 2
        right = w - 1 - left
        lo = jnp.clip(pos - left, 0.0, float(seq))
        hi = jnp.clip(pos + (right + 1), 0.0, float(seq))
        cnt = hi - lo
        ug = ub_ref[:, gi * gc:(gi + 1) * gc]
        wsum = jnp.dot(band_ref[gi], ug, preferred_element_type=F32)
        p = wsum / cnt - ug[HALO:HALO + tm, :].astype(F32)
        y = jnp.dot(p.astype(BF16), poolw_ref[gi], preferred_element_type=F32)
        ya_ref[:, gi * dout:(gi + 1) * dout] = (
            y * pscale_ref[:, gi * dout:(gi + 1) * dout]).astype(BF16)

    a = ub_ref[:, mw:2 * mw].astype(F32)
    gate = ub_ref[:, 2 * mw:3 * mw].astype(F32)
    vb_ref[...] = a * _sigmoid(gate)
    rc = 128
    base = HALO - CONF_KERNEL // 2 - 1
    assert base % 8 == 0 and sh_ref.shape[1] >= tm + 8 * (CONF_KERNEL // 8)
    nsh = sh_ref.shape[1]
    for c0 in range(0, mw, LANES):
        for r in range(1, 8):
            sh_ref[r] = vb_ref[base + r:base + r + nsh, c0:c0 + LANES]
        for r0 in range(0, tm, rc):
            acc = jnp.zeros((rc, LANES), F32)
            for k in range(CONF_KERNEL):
                q, r = divmod(k + 1, 8)
                row = r0 + 8 * q
                if r == 0:
                    src = vb_ref[base + row:base + row + rc, c0:c0 + LANES]
                else:
                    src = sh_ref[r, row:row + rc, :]
                acc = acc + dww_ref[k:k + 1, c0:c0 + LANES] * src
            cv_ref[r0:r0 + rc, c0:c0 + LANES] = acc
    v = cv_ref[...] + dwb_ref[...]
    mu = jnp.mean(v, axis=-1, keepdims=True)
    dv = v - mu
    var = jnp.mean(dv * dv, axis=-1, keepdims=True)
    ln = dv * lax.rsqrt(var + EPS) * lng_ref[...] + lnb_ref[...]
    act = ln * _sigmoid(ln)
    yb_ref[...] = jnp.dot(act.astype(BF16), cwo_ref[...], preferred_element_type=F32).astype(BF16)

    pb_ref[...] = ub_ref[:, 4 * mw:5 * mw].astype(F32) * ub_ref[:, 5 * mw:6 * mw].astype(F32)
    z = jnp.zeros((tm, mw), F32)
    for k in range(SC_KERNEL):
        off = HALO - SC_KERNEL // 2 + k
        z = z + scw_ref[k:k + 1, :] * pb_ref[off:off + tm, :]
    bg = ub_ref[HALO:HALO + tm, 3 * mw:4 * mw].astype(F32)
    yc_ref[...] = jnp.dot((bg * z).astype(BF16), swo_ref[...], preferred_element_type=F32).astype(BF16)


def _mixers(u, band, pool_w, pool_scale, dw_w, dw_b, ln_g, ln_b, conf_wo, sc_w, sc_wo, seq, d):
    t = u.shape[0]
    mw = pool_w.shape[0] * pool_w.shape[1]
    tm = band.shape[1]
    tps = seq // tm
    hb = tm // HALO
    nhalo = t // HALO
    wabc = 6 * mw
    full = lambda shape: pl.BlockSpec(shape, lambda i: (0,) * len(shape))
    out_sds = jax.ShapeDtypeStruct((t, d), BF16)
    return pl.pallas_call(
        functools.partial(_mixers_kernel, tm=tm, tps=tps, seq=seq, mw=mw),
        grid=(t // tm,),
        in_specs=[
            pl.BlockSpec((tm, wabc), lambda i: (i, 0)),
            pl.BlockSpec((HALO, wabc), lambda i: (jnp.maximum(i * hb - 1, 0), 0)),
            pl.BlockSpec((HALO, wabc), lambda i: (jnp.minimum((i + 1) * hb, nhalo - 1), 0)),
            full(band.shape), full(pool_w.shape), full((1, d)),
            full(dw_w.shape), full((1, mw)), full((1, mw)), full((1, mw)),
            full(conf_wo.shape), full(sc_w.shape), full(sc_wo.shape),
        ],
        out_specs=[pl.BlockSpec((tm, d), lambda i: (i, 0))] * 3,
        out_shape=[out_sds] * 3,
        scratch_shapes=[
            pltpu.VMEM((tm + 2 * HALO, wabc), BF16),
            pltpu.VMEM((tm + 2 * HALO, mw), F32),
            pltpu.VMEM((tm + 2 * HALO, mw), F32),
            pltpu.VMEM((tm, mw), F32),
            pltpu.VMEM((8, tm + 32, LANES), F32),
        ],
        compiler_params=_cparams("arbitrary"),
        name="mixers",
    )(u, u, u, band, pool_w, pool_scale.reshape(1, d), dw_w, dw_b.reshape(1, mw),
      ln_g.reshape(1, mw), ln_b.reshape(1, mw), conf_wo, sc_w, sc_wo)


def _dft_tables(seq, gc):
    n1, n2 = FFT_N1, seq // FFT_N1
    k2 = np.arange(n2)[:, None] * np.arange(n2)[None, :] * (2 * np.pi / n2)
    f2 = np.concatenate([np.cos(k2), -np.sin(k2)], axis=0)
    k1 = np.arange(n1)[:, None] * np.arange(n1)[None, :] * (2 * np.pi / n1)
    c1, s1 = np.cos(k1), np.sin(k1)
    m1 = np.block([[c1, s1], [-s1, c1]])
    tw = np.arange(n2)[:, None] * np.arange(n1)[None, :] * (2 * np.pi / seq)
    twc = np.broadcast_to(np.cos(tw)[:, :, None], (n2, n1, LANES))
    tws = np.broadcast_to(np.sin(tw)[:, :, None], (n2, n1, LANES))
    kc = np.arange(gc)[:, None] * np.arange(gc)[None, :] * (2 * np.pi / gc)
    mc = np.concatenate([np.cos(kc), np.sin(kc)], axis=0) / np.sqrt(seq * gc)
    return (jnp.asarray(f2, BF16), jnp.asarray(m1, BF16), jnp.asarray(twc, F32),
            jnp.asarray(tws, F32), jnp.asarray(mc, BF16))


def _fft1_kernel(f2_ref, x_ref, a_ref):
    a_ref[0] = jnp.dot(f2_ref[...], x_ref[0], preferred_element_type=F32).astype(BF16)


def _fft2_kernel(a_ref, twc_ref, tws_ref, m1_ref, mc_ref, wo_ref, y_ref, f_ref, *, kb, gc, d):
    n1 = m1_ref.shape[0] // 2
    mw = a_ref.shape[-1]
    cw = 512
    for j in range(kb):
        ar = a_ref[0, 0, j].astype(F32)
        ai = a_ref[0, 1, j].astype(F32)
        tc = jnp.tile(twc_ref[j], (1, mw // LANES))
        ts = jnp.tile(tws_ref[j], (1, mw // LANES))
        br = ar * tc + ai * ts
        bi = ai * tc - ar * ts
        bb = jnp.concatenate([br, bi], axis=0).astype(BF16)
        xx = jnp.dot(m1_ref[...], bb, preferred_element_type=F32)
        xr = xx[0:n1].astype(BF16)
        xi = xx[n1:2 * n1].astype(BF16)
        for g in range(mw // gc):
            lhs = jnp.concatenate([xr[:, g * gc:(g + 1) * gc], xi[:, g * gc:(g + 1) * gc]], axis=1)
            f_ref[g, pl.ds(j, n1, stride=kb), :] = jnp.dot(
                lhs, mc_ref[...], preferred_element_type=F32)
    f = jnp.concatenate([f_ref[g] for g in range(mw // gc)], axis=1).astype(BF16)
    for c0 in range(0, d, cw):
        yc = jnp.dot(f, wo_ref[:, c0:c0 + cw], preferred_element_type=F32)
        y_ref[0, :, :, c0:c0 + cw] = yc.reshape(n1, kb, cw).astype(BF16)


def _fourier(ud, tables, w_out, nb, seq, gc):
    f2, m1, twc, tws, mc = tables
    mw = ud.shape[1]
    d = w_out.shape[1]
    n1, n2 = FFT_N1, seq // FFT_N1
    xv = ud.reshape(nb, n2, n1 * mw)
    tn = 8192
    a = pl.pallas_call(
        _fft1_kernel,
        grid=(nb, n1 * mw // tn),
        in_specs=[
            pl.BlockSpec((2 * n2, n2), lambda b, j: (0, 0)),
            pl.BlockSpec((1, n2, tn), lambda b, j: (b, 0, j)),
        ],
        out_specs=pl.BlockSpec((1, 2 * n2, tn), lambda b, j: (b, 0, j)),
        out_shape=jax.ShapeDtypeStruct((nb, 2 * n2, n1 * mw), BF16),
        compiler_params=_cparams("arbitrary", "arbitrary"),
        name="fft_stage1",
    )(f2, xv)
    a5 = a.reshape(nb, 2, n2, n1, mw)
    kb = 16
    y = pl.pallas_call(
        functools.partial(_fft2_kernel, kb=kb, gc=gc, d=d),
        grid=(nb, n2 // kb),
        in_specs=[
            pl.BlockSpec((1, 2, kb, n1, mw), lambda b, k: (b, 0, k, 0, 0)),
            pl.BlockSpec((kb, n1, LANES), lambda b, k: (k, 0, 0)),
            pl.BlockSpec((kb, n1, LANES), lambda b, k: (k, 0, 0)),
            pl.BlockSpec(m1.shape, lambda b, k: (0, 0)),
            pl.BlockSpec(mc.shape, lambda b, k: (0, 0)),
            pl.BlockSpec(w_out.shape, lambda b, k: (0, 0)),
        ],
        out_specs=pl.BlockSpec((1, n1, kb, d), lambda b, k: (b, 0, k, 0)),
        out_shape=jax.ShapeDtypeStruct((nb, n1, n2, d), BF16),
        scratch_shapes=[pltpu.VMEM((mw // gc, n1 * kb, gc), F32)],
        compiler_params=_cparams("arbitrary", "arbitrary"),
        name="fft_stage2",
    )(a5, twc, tws, m1, mc, w_out)
    return y.reshape(nb * seq, d)


def _merge_kernel(x_ref, h_ref, gt_ref, ya_ref, yb_ref, yc_ref, yd_ref,
                  wa_ref, wb_ref, wc_ref, wd_ref, wo_ref, o_ref, acc_ref):
    j = pl.program_id(1)

    @pl.when(j == 0)
    def _():
        acc_ref[...] = jnp.zeros_like(acc_ref)

    h = h_ref[...]
    merged = None
    for w_ref, y_ref in ((wa_ref, ya_ref), (wb_ref, yb_ref), (wc_ref, yc_ref), (wd_ref, yd_ref)):
        gl = jnp.dot(h, w_ref[0], preferred_element_type=F32)
        term = _sigmoid(gl) * y_ref[...].astype(F32)
        merged = term if merged is None else merged + term
    acc_ref[...] += jnp.dot(merged.astype(BF16), wo_ref[0], preferred_element_type=F32)

    @pl.when(j == pl.num_programs(1) - 1)
    def _():
        o_ref[...] = x_ref[...] + gt_ref[0] * acc_ref[...]


def _merge(x2, h, gt, ys, w_in_b, layer, off_g, w_o_b, seq):
    t, d = x2.shape
    tm, tc = 512, 512
    tps = seq // tm
    nj = d // tc
    gb = off_g // tc
    mod_spec = pl.BlockSpec((1, 1, d), lambda i, j: (i // tps, 0, 0))
    y_spec = pl.BlockSpec((tm, tc), lambda i, j: (i, j))
    w_specs = [pl.BlockSpec((1, d, tc),
                            functools.partial(lambda i, j, k: (layer, 0, gb + k * nj + j), k=k))
               for k in range(N_BRANCH)]
    return pl.pallas_call(
        _merge_kernel,
        grid=(t // tm, nj),
        in_specs=[
            pl.BlockSpec((tm, d), lambda i, j: (i, 0)),
            pl.BlockSpec((tm, d), lambda i, j: (i, 0)),
            mod_spec,
            y_spec, y_spec, y_spec, y_spec,
            *w_specs,
            pl.BlockSpec((1, tc, d), lambda i, j: (layer, j, 0)),
        ],
        out_specs=pl.BlockSpec((tm, d), lambda i, j: (i, 0)),
        out_shape=jax.ShapeDtypeStruct((t, d), F32),
        scratch_shapes=[pltpu.VMEM((tm, d), F32)],
        compiler_params=_cparams("arbitrary", "arbitrary"),
        name="merge",
    )(x2, h, gt, *ys, w_in_b, w_in_b, w_in_b, w_in_b, w_o_b)


ROW_SUB = 8


def _store_rows(ref, v):
    m, d = v.shape
    half = d // 2
    for s in range(ROW_SUB):
        hi = lax.bitcast_convert_type(
            v[:, s * LANES:(s + 1) * LANES].astype(BF16).astype(F32), U32)
        lo = lax.bitcast_convert_type(
            v[:, half + s * LANES:half + (s + 1) * LANES].astype(BF16).astype(F32), U32)
        ref[pl.ds(s, m, stride=ROW_SUB), :] = hi | (lo >> 16)


def _load_rows(ref, m, s):
    p = ref[pl.ds(s, m, stride=ROW_SUB), :]
    hi = lax.bitcast_convert_type(p & jnp.uint32(0xFFFF0000), F32)
    lo = lax.bitcast_convert_type(p << 16, F32)
    return hi, lo


def _router_kernel(x_ref, g_ref, sh_ref, sc_ref, whi_ref, wlo_ref, b_ref,
                   hp_ref, meta_ref, rf_ref, cnt_ref, base_ref, *, tm):
    i = pl.program_id(0)

    @pl.when(i == 0)
    def _():
        base_ref[...] = jnp.zeros_like(base_ref)

    h = _norm_modulate(x_ref[...], g_ref[...], sh_ref[0], sc_ref[0])
    _store_rows(hp_ref, h)
    h_hi = h.astype(BF16)
    h_lo = (h - h_hi.astype(F32)).astype(BF16)
    whi = whi_ref[...]
    logits = (jnp.dot(h_hi, whi, preferred_element_type=F32)
              + jnp.dot(h_lo, whi, preferred_element_type=F32)
              + jnp.dot(h_hi, wlo_ref[...], preferred_element_type=F32)) + b_ref[...]

    lane = lax.broadcasted_iota(I32, (tm, LANES), 1)
    lane_f = lane.astype(F32)
    is_grp = (lane >= N_EXPERTS) & (lane < N_EXPERTS + N_GROUPS)
    lg = jnp.where(is_grp, logits, NEG)
    mg = jnp.max(lg, axis=1, keepdims=True)
    gidx = jnp.min(jnp.where(lg == mg, lane_f - N_EXPERTS, 1e9), axis=1, keepdims=True)
    pg_top = 1.0 / jnp.sum(jnp.where(is_grp, jnp.exp(lg - mg), 0.0), axis=1, keepdims=True)

    in_grp = (lane < N_EXPERTS) & (jnp.right_shift(lane, 3).astype(F32) == gidx)
    le = jnp.where(in_grp, logits, NEG)
    m1 = jnp.max(le, axis=1, keepdims=True)
    e1 = jnp.min(jnp.where(le == m1, lane_f, 1e9), axis=1, keepdims=True)
    le2 = jnp.where(lane_f == e1, NEG, le)
    m2 = jnp.max(le2, axis=1, keepdims=True)
    e2 = jnp.min(jnp.where(le2 == m2, lane_f, 1e9), axis=1, keepdims=True)
    w1 = 1.0 / (1.0 + jnp.exp(m2 - m1))
    w2 = 1.0 - w1

    is1 = lane_f == e1
    is2 = lane_f == e2
    oh = jnp.where(is1 | is2, 1.0, 0.0)
    r_io = lax.broadcasted_iota(I32, (tm, tm), 0)
    c_io = lax.broadcasted_iota(I32, (tm, tm), 1)
    tri = jnp.where(r_io > c_io, 1.0, 0.0).astype(BF16)
    prior = jnp.dot(tri, oh.astype(BF16), preferred_element_type=F32) + base_ref[0:1, :]
    pos1 = jnp.sum(jnp.where(is1, prior, 0.0), axis=1, keepdims=True)
    pos2 = jnp.sum(jnp.where(is2, prior, 0.0), axis=1, keepdims=True)
    base_ref[0:1, :] = base_ref[0:1, :] + jnp.sum(oh, axis=0, keepdims=True)

    ints = jnp.where(lane == 0, e1, jnp.where(lane == 1, e2, jnp.where(lane == 2, pos1, pos2)))
    meta_ref[...] = jnp.transpose(ints)[0:8, :]
    rf_ref[...] = jnp.where(lane == 0, w1 * pg_top, w2 * pg_top)
    cnt_ref[...] = base_ref[...]


def _router(x2, g, sh, sc, w_hi, w_lo, bias, seq):
    t, d = x2.shape
    tm = 512
    tps = seq // tm
    mod_spec = pl.BlockSpec((1, 1, d), lambda i: (i // tps, 0, 0))
    row_spec = pl.BlockSpec((tm, LANES), lambda i: (i, 0))
    return pl.pallas_call(
        functools.partial(_router_kernel, tm=tm),
        grid=(t // tm,),
        in_specs=[
            pl.BlockSpec((tm, d), lambda i: (i, 0)),
            pl.BlockSpec((1, d), lambda i: (0, 0)),
            mod_spec, mod_spec,
            pl.BlockSpec((d, LANES), lambda i: (0, 0)),
            pl.BlockSpec((d, LANES), lambda i: (0, 0)),
            pl.BlockSpec((1, LANES), lambda i: (0, 0)),
        ],
        out_specs=[pl.BlockSpec((tm * ROW_SUB, LANES), lambda i: (i, 0)),
                   pl.BlockSpec((8, tm), lambda i: (0, i)),
                   row_spec,
                   pl.BlockSpec((8, LANES), lambda i: (0, 0))],
        out_shape=[jax.ShapeDtypeStruct((t * ROW_SUB, LANES), U32),
                   jax.ShapeDtypeStruct((8, t), F32),
                   jax.ShapeDtypeStruct((t, LANES), F32),
                   jax.ShapeDtypeStruct((8, LANES), F32)],
        scratch_shapes=[pltpu.VMEM((8, LANES), F32)],
        compiler_params=_cparams("arbitrary"),
        name="router",
    )(x2, g, sh, sc, w_hi, w_lo, bias)


def _dispatch_kernel(d1_ref, d2_ref, h_ref, xb_in_ref, xb_ref, sem, *, tm):
    del xb_in_ref
    base = pl.program_id(0) * tm

    def row_copy(r, dst):
        return pltpu.make_async_copy(
            h_ref.at[pl.ds(pl.multiple_of(r * ROW_SUB, ROW_SUB), ROW_SUB), :],
            xb_ref.at[pl.ds(pl.multiple_of(dst * ROW_SUB, ROW_SUB), ROW_SUB), :], sem)

    def issue(r, carry):
        row_copy(r, d1_ref[base + r]).start(priority=0)
        row_copy(r, d2_ref[base + r]).start(priority=1)
        return carry

    lax.fori_loop(0, tm, issue, 0, unroll=8)

    def drain(r, carry):
        row_copy(r, 0).wait()
        row_copy(r, 0).wait()
        return carry

    lax.fori_loop(0, tm, drain, 0, unroll=8)


def _dispatch(dest1, dest2, hp, n_rows):
    t = hp.shape[0] // ROW_SUB
    tm = 512
    xb0 = jnp.zeros((n_rows * ROW_SUB, LANES), U32)
    return pl.pallas_call(
        functools.partial(_dispatch_kernel, tm=tm),
        grid_spec=pltpu.PrefetchScalarGridSpec(
            num_scalar_prefetch=2,
            grid=(t // tm,),
            in_specs=[pl.BlockSpec((tm * ROW_SUB, LANES), lambda i, d1, d2: (i, 0)),
                      pl.BlockSpec(memory_space=pl.ANY)],
            out_specs=pl.BlockSpec(memory_space=pl.ANY),
            scratch_shapes=[pltpu.SemaphoreType.DMA(())],
        ),
        out_shape=jax.ShapeDtypeStruct((n_rows * ROW_SUB, LANES), U32),
        input_output_aliases={3: 0},
        compiler_params=_cparams("arbitrary"),
        name="dispatch",
    )(dest1, dest2, hp, xb0)


def _experts_kernel(be_ref, nxt_ref, slot_ref, nu_ref, xb_ref, w1_hbm, w3_hbm, w2_hbm, yb_ref,
                    w1f, w3f, w2f, w1b, w3b, w2b, sem, *, layer):
    b = pl.program_id(0)
    e = be_ref[b]
    slot = slot_ref[b]
    first = (b == 0) | (e != be_ref[jnp.maximum(b - 1, 0)])

    def fetch(exp, s):
        return (pltpu.make_async_copy(w1_hbm.at[layer, exp], w1f.at[s], sem.at[s, 0]),
                pltpu.make_async_copy(w3_hbm.at[layer, exp], w3f.at[s], sem.at[s, 1]),
                pltpu.make_async_copy(w2_hbm.at[layer, exp], w2f.at[s], sem.at[s, 2]))

    @pl.when(b == 0)
    def _():
        for cp in fetch(e, slot):
            cp.start()

    @pl.when(first)
    def _():
        for cp in fetch(e, slot):
            cp.wait()
        w1b[...] = w1f[slot].astype(BF16)
        w3b[...] = w3f[slot].astype(BF16)
        w2b[...] = w2f[slot].astype(BF16)

        @pl.when(nxt_ref[b] >= 0)
        def _():
            for cp in fetch(nxt_ref[b], 1 - slot):
                cp.start()

    @pl.when(b < nu_ref[0])
    def _():
        bm = xb_ref.shape[0] // ROW_SUB
        parts = [_load_rows(xb_ref, bm, s) for s in range(ROW_SUB)]
        xrow = jnp.concatenate([p[0].astype(BF16) for p in parts]
                               + [p[1].astype(BF16) for p in parts], axis=1)
        h1 = jnp.dot(xrow, w1b[...], preferred_element_type=F32)
        h3 = jnp.dot(xrow, w3b[...], preferred_element_type=F32)
        hh = (h1 * _sigmoid(h1) * h3).astype(BF16)
        _store_rows(yb_ref, jnp.dot(hh, w2b[...], preferred_element_type=F32))

    @pl.when(b >= nu_ref[0])
    def _():
        yb_ref[...] = jnp.zeros_like(yb_ref)


def _experts(block_exp, next_exp, slot, n_used, xb, w1, w3, w2, layer):
    n_rows = xb.shape[0] // ROW_SUB
    _, _, d, de = w1.shape
    bm = EXPERT_ROWS
    row_spec = pl.BlockSpec((bm * ROW_SUB, LANES), lambda b, *_: (b, 0))
    hbm_spec = pl.BlockSpec(memory_space=pl.ANY)
    return pl.pallas_call(
        functools.partial(_experts_kernel, layer=layer),
        grid_spec=pltpu.PrefetchScalarGridSpec(
            num_scalar_prefetch=4,
            grid=(n_rows // bm,),
            in_specs=[row_spec, hbm_spec, hbm_spec, hbm_spec],
            out_specs=row_spec,
            scratch_shapes=[pltpu.VMEM((2, d, de), F32), pltpu.VMEM((2, d, de), F32),
                            pltpu.VMEM((2, de, d), F32),
                            pltpu.VMEM((d, de), BF16), pltpu.VMEM((d, de), BF16),
                            pltpu.VMEM((de, d), BF16),
                            pltpu.SemaphoreType.DMA((2, 3))],
        ),
        out_shape=jax.ShapeDtypeStruct((n_rows * ROW_SUB, LANES), U32),
        compiler_params=_cparams("arbitrary"),
        name="experts",
    )(block_exp, next_exp, slot, n_used, xb, w1, w3, w2)


def _combine_kernel(d1_ref, d2_ref, x_ref, rf_ref, gt_ref, fg_ref, yb_ref, o_ref,
                    r1_ref, r2_ref, sem, *, tm, final):
    base = pl.program_id(0) * tm

    def row_copy(src, buf, r):
        return pltpu.make_async_copy(
            yb_ref.at[pl.ds(pl.multiple_of(src * ROW_SUB, ROW_SUB), ROW_SUB), :],
            buf.at[pl.ds(pl.multiple_of(r * ROW_SUB, ROW_SUB), ROW_SUB), :], sem)

    def issue(r, carry):
        row_copy(d1_ref[base + r], r1_ref, r).start(priority=0)
        row_copy(d2_ref[base + r], r2_ref, r).start(priority=1)
        return carry

    lax.fori_loop(0, tm, issue, 0, unroll=8)

    def drain(r, carry):
        row_copy(0, r1_ref, r).wait()
        row_copy(0, r2_ref, r).wait()
        return carry

    lax.fori_loop(0, tm, drain, 0, unroll=8)

    d = x_ref.shape[1]
    half = d // 2
    w1 = rf_ref[:, 0:1]
    w2 = rf_ref[:, 1:2]
    ms = jnp.zeros((tm, 1), F32)
    for s in range(ROW_SUB):
        a_hi, a_lo = _load_rows(r1_ref, tm, s)
        b_hi, b_lo = _load_rows(r2_ref, tm, s)
        for c0, ya, yb2 in ((s * LANES, a_hi, b_hi), (half + s * LANES, a_lo, b_lo)):
            xn = x_ref[:, c0:c0 + LANES] + gt_ref[0, :, c0:c0 + LANES] * (ya * w1 + yb2 * w2)
            o_ref[:, c0:c0 + LANES] = xn
            if final:
                ms = ms + jnp.sum(xn * xn, axis=-1, keepdims=True)
    if final:
        o_ref[...] = o_ref[...] * lax.rsqrt(ms / d + EPS) * fg_ref[...]


def _combine(dest1, dest2, x2, rf, gt, final_g, yb, seq, final):
    t, d = x2.shape
    hw = ROW_SUB * LANES
    tm = 512
    tps = seq // tm
    return pl.pallas_call(
        functools.partial(_combine_kernel, tm=tm, final=final),
        grid_spec=pltpu.PrefetchScalarGridSpec(
            num_scalar_prefetch=2,
            grid=(t // tm,),
            in_specs=[
                pl.BlockSpec((tm, d), lambda i, d1, d2: (i, 0)),
                pl.BlockSpec((tm, LANES), lambda i, d1, d2: (i, 0)),
                pl.BlockSpec((1, 1, d), lambda i, d1, d2: (i // tps, 0, 0)),
                pl.BlockSpec((1, d), lambda i, d1, d2: (0, 0)),
                pl.BlockSpec(memory_space=pl.ANY),
            ],
            out_specs=pl.BlockSpec((tm, d), lambda i, d1, d2: (i, 0)),
            scratch_shapes=[pltpu.VMEM((tm * ROW_SUB, LANES), U32),
                            pltpu.VMEM((tm * ROW_SUB, LANES), U32),
                            pltpu.SemaphoreType.DMA(())],
        ),
        out_shape=jax.ShapeDtypeStruct((t, d), F32),
        compiler_params=_cparams("arbitrary"),
        name="combine",
    )(dest1, dest2, x2, rf, gt, final_g.reshape(1, d), yb)


def _routing_tables(meta, cnt, n_blocks):
    bm = EXPERT_ROWS
    counts = cnt[0, :N_EXPERTS].astype(I32)
    pcounts = (counts + bm - 1) // bm * bm
    pends = jnp.cumsum(pcounts)
    pstarts = pends - pcounts
    eids = jnp.arange(N_EXPERTS, dtype=I32)
    e1, e2, pos1, pos2 = [meta[k].astype(I32) for k in range(4)]
    dest1 = pos1 + jnp.sum(jnp.where(e1[:, None] == eids[None, :], pstarts[None, :], 0), axis=1)
    dest2 = pos2 + jnp.sum(jnp.where(e2[:, None] == eids[None, :], pstarts[None, :], 0), axis=1)
    first_row = jnp.arange(n_blocks, dtype=I32) * bm
    block_exp = jnp.minimum(
        jnp.sum((pends[None, :] <= first_row[:, None]).astype(I32), axis=1), N_EXPERTS - 1)
    n_used = pends[-1:] // bm
    blk = jnp.arange(n_blocks, dtype=I32)
    block_exp = jnp.where(blk < n_used[0], block_exp, jnp.max(jnp.where(counts > 0, eids, 0)))
    first = jnp.concatenate([jnp.ones((1,), bool), block_exp[1:] != block_exp[:-1]])
    slot = (jnp.cumsum(first.astype(I32)) - 1) % 2
    later = jnp.where(block_exp[None, :] > block_exp[:, None], block_exp[None, :], N_EXPERTS)
    next_exp = jnp.min(later, axis=1)
    next_exp = jnp.where(next_exp == N_EXPERTS, -1, next_exp)
    return (dest1, dest2, block_exp.astype(I32), next_exp.astype(I32), slot.astype(I32),
            n_used.astype(I32))


def kernel(x, c, ada_w, ada_b, norm1_g, w_in, pool_w, pool_scale, conf_dw_w, conf_dw_b,
           conf_ln_g, conf_ln_b, conf_w_out, sc_conv_w, sc_w_out, fnet_w_out, w_o, norm2_g,
           rg_w, rg_b, re_w, re_b, exp_w1, exp_w3, exp_w2, final_g):
    nb, seq, d = x.shape
    depth = ada_w.shape[0]
    t = nb * seq
    mw = d // N_BRANCH
    off_d = 6 * mw
    off_g = 7 * mw
    gc = mw // N_GROUPS
    n_rows = t * 2 + N_EXPERTS * EXPERT_ROWS
    n_blocks = n_rows // EXPERT_ROWS

    mod = _ada_mod(c, ada_w, ada_b)
    band = jnp.asarray(_pool_band(512), BF16)
    tables = _dft_tables(seq, gc)
    x2 = x.reshape(t, d)
    w_in_b = w_in.astype(BF16)
    w_o_b = w_o.astype(BF16)
    assert d // 2 == ROW_SUB * LANES

    for l in range(depth):
        sh1, sc1, gt1, sh2, sc2, gt2 = [mod[l, :, k * d:(k + 1) * d].reshape(nb, 1, d)
                                        for k in range(6)]
        g1 = norm1_g[l].reshape(1, d)
        u, h = _inproj(x2, g1, sh1, sc1, w_in_b, l, off_g, seq)
        ya, yb, yc = _mixers(u, band, pool_w[l].astype(BF16), pool_scale[l], conf_dw_w[l],
                             conf_dw_b[l], conf_ln_g[l], conf_ln_b[l],
                             conf_w_out[l].astype(BF16), sc_conv_w[l],
                             sc_w_out[l].astype(BF16), seq, d)
        yd = _fourier(u[:, off_d:off_g], tables, fnet_w_out[l].astype(BF16), nb, seq, gc)
        x2 = _merge(x2, h, gt1, (ya, yb, yc, yd), w_in_b, l, off_g, w_o_b, seq)

        wr = jnp.zeros((d, LANES), F32).at[:, :N_EXPERTS].set(re_w[l])
        wr = wr.at[:, N_EXPERTS:N_EXPERTS + N_GROUPS].set(rg_w[l])
        br = jnp.zeros((1, LANES), F32).at[0, :N_EXPERTS].set(re_b[l])
        br = br.at[0, N_EXPERTS:N_EXPERTS + N_GROUPS].set(rg_b[l])
        wr_hi = wr.astype(BF16)
        wr_lo = (wr - wr_hi.astype(F32)).astype(BF16)
        hp, meta, rf, cnt = _router(x2, norm2_g[l].reshape(1, d), sh2, sc2, wr_hi, wr_lo, br, seq)
        dest1, dest2, block_exp, next_exp, slot, n_used = _routing_tables(meta, cnt, n_blocks)
        xb = _dispatch(dest1, dest2, hp, n_rows)
        ybk = _experts(block_exp, next_exp, slot, n_used, xb, exp_w1, exp_w3, exp_w2, l)
        x2 = _combine(dest1, dest2, x2, rf, gt2, final_g, ybk, seq, final=(l == depth - 1))
    return x2.reshape(nb, seq, d)
```
